```python
import math
import jax, jax.numpy as jnp
from jax import lax
import numpy as np

D_MODEL = 1024
BATCH = 16
SEQ = 4096
DEPTH = 1

ATTN_HEADS = 8
ATTN_HEAD_DIM = 64
ATTN_W = ATTN_HEADS * ATTN_HEAD_DIM
LRU_W = D_MODEL - ATTN_W
LRU_BLOCKS = 8
LRU_BW = LRU_W // LRU_BLOCKS
LRU_C = 8.0
CONV_W = 4
IN_COLS = 3 * ATTN_W + 2 * LRU_W
Q_BLOCK = 128
PEER_HEADS = 8
PEER_DK = 128
PEER_DH = PEER_DK // 2
N_KEYS = 128
N_EXPERTS = N_KEYS * N_KEYS
PEER_TOPK = 16
PEER_CHUNK = 128
EPS = 1e-6

kernel_name = "hymba_sbattn_rglru_peer_block"


def _rmsnorm(x, g):
    xf = x.astype(jnp.float32)
    y = xf * lax.rsqrt(jnp.mean(xf * xf, axis=-1, keepdims=True) + EPS)
    return y * g.astype(jnp.float32)


def _stick_breaking_attention(q, k, v):
    S = q.shape[2]
    scale = 1.0 / math.sqrt(q.shape[-1])
    outs = []
    for qb in range(S // Q_BLOCK):
        t0 = qb * Q_BLOCK
        t1 = t0 + Q_BLOCK
        qi = q[:, :, t0:t1]
        kp = k[:, :, :t1]
        vp = v[:, :, :t1]
        z = jnp.einsum('bhqd,bhkd->bhqk', qi, kp) * scale
        mask = jnp.arange(t1)[None, :] < jnp.arange(t0, t1)[:, None]
        log_fail = jnp.where(mask, jax.nn.log_sigmoid(-z), 0.0)
        later = lax.cumsum(log_fail, axis=3, reverse=True) - log_fail
        w = jnp.where(mask, jnp.exp(jax.nn.log_sigmoid(z) + later), 0.0)
        outs.append(jnp.einsum('bhqk,bhkd->bhqd', w, vp))
    return jnp.concatenate(outs, axis=2)


def _causal_depthwise_conv(x, w, b):
    S = x.shape[1]
    xp = jnp.pad(x, ((0, 0), (CONV_W - 1, 0), (0, 0)))
    y = b
    for tap in range(CONV_W):
        y = y + xp[:, tap:tap + S, :] * w[tap]
    return y


def _rg_lru(xb, w_rg, b_rg, w_ig, b_ig, lam):
    B, S, W = xb.shape
    xh = xb.reshape(B, S, LRU_BLOCKS, LRU_BW)
    r = jax.nn.sigmoid(jnp.einsum('bshi,hij->bshj', xh, w_rg).reshape(B, S, W) + b_rg)
    i = jax.nn.sigmoid(jnp.einsum('bshi,hij->bshj', xh, w_ig).reshape(B, S, W) + b_ig)
    log_a = -LRU_C * r * jax.nn.softplus(-lam)
    a = jnp.exp(log_a)
    u = jnp.sqrt(-jnp.expm1(2.0 * log_a)) * (i * xb)

    def combine(left, right):
        a1, b1 = left
        a2, b2 = right
        return a1 * a2, a2 * b1 + b2

    _, h = lax.associative_scan(combine, (a, u), axis=1)
    return h


def _peer(h, w_pq, sub_keys1, sub_keys2, expert_u, expert_v):
    B, S, D = h.shape
    tokens = h.reshape(B * S // PEER_CHUNK, PEER_CHUNK, D)

    def chunk_fn(xc):
        T = xc.shape[0]
        q = (xc @ w_pq).reshape(T, PEER_HEADS, PEER_DK)
        s1 = jnp.einsum('thd,hnd->thn', q[..., :PEER_DH], sub_keys1)
        s2 = jnp.einsum('thd,hnd->thn', q[..., PEER_DH:], sub_keys2)
        v1, i1 = lax.top_k(s1, PEER_TOPK)
        v2, i2 = lax.top_k(s2, PEER_TOPK)
        cand = (v1[..., :, None] + v2[..., None, :]).reshape(T, PEER_HEADS, PEER_TOPK * PEER_TOPK)
        cidx = (i1[..., :, None] * N_KEYS + i2[..., None, :]).reshape(T, PEER_HEADS, PEER_TOPK * PEER_TOPK)
        top_s, pos = lax.top_k(cand, PEER_TOPK)
        eidx = jnp.take_along_axis(cidx, pos, axis=-1)
        g = jax.nn.softmax(top_s.astype(jnp.float32), axis=-1)
        u_sel = jnp.take(expert_u, eidx, axis=0)
        act = jax.nn.gelu(jnp.einsum('thkd,td->thk', u_sel, xc))
        v_sel = jnp.take(expert_v, eidx, axis=0)
        return jnp.einsum('thk,thkd->td', g * act, v_sel)

    return lax.map(chunk_fn, tokens).reshape(B, S, D)


def setup_inputs(seed: int = 0) -> dict:
    key = jax.random.key(seed)
    ks = jax.random.split(key, 26)
    f32 = jnp.float32

    def nrm(k, shape, scale):
        return jax.random.normal(k, shape, f32) * scale

    L = DEPTH
    lam_u = jax.random.uniform(ks[12], (L, LRU_W), f32, minval=0.9, maxval=0.999)
    a0 = lam_u ** (1.0 / LRU_C)
    lru_lambda = jnp.log(a0) - jnp.log1p(-a0)
    return {
        "x": nrm(ks[0], (BATCH, SEQ, D_MODEL), 1.0),
        "c": nrm(ks[1], (BATCH, D_MODEL), 1.0),
        "w_mod": nrm(ks[2], (L, D_MODEL, 6 * D_MODEL), D_MODEL ** -0.5),
        "b_mod": nrm(ks[3], (L, 6 * D_MODEL), 0.01),
        "g_norm1": 1.0 + nrm(ks[4], (L, D_MODEL), 0.02),
        "w_in": nrm(ks[5], (L, D_MODEL, IN_COLS), D_MODEL ** -0.5),
        "g_q": 1.0 + nrm(ks[6], (L, ATTN_HEAD_DIM), 0.02),
        "g_k": 1.0 + nrm(ks[7], (L, ATTN_HEAD_DIM), 0.02),
        "conv_w": nrm(ks[8], (L, CONV_W, LRU_W), CONV_W ** -0.5),
        "conv_b": nrm(ks[9], (L, LRU_W), 0.01),
        "w_rg": nrm(ks[10], (L, LRU_BLOCKS, LRU_BW, LRU_BW), LRU_BW ** -0.5),
        "b_rg": nrm(ks[11], (L, LRU_W), 0.01),
        "w_ig": nrm(ks[13], (L, LRU_BLOCKS, LRU_BW, LRU_BW), LRU_BW ** -0.5),
        "b_ig": nrm(ks[14], (L, LRU_W), 0.01),
        "lru_lambda": lru_lambda,
        "g_out_attn": 1.0 + nrm(ks[15], (L, ATTN_W), 0.02),
        "g_out_lru": 1.0 + nrm(ks[16], (L, LRU_W), 0.02),
        "w_out": nrm(ks[17], (L, D_MODEL, D_MODEL), D_MODEL ** -0.5),
        "g_norm2": 1.0 + nrm(ks[18], (L, D_MODEL), 0.02),
        "w_pq": nrm(ks[19], (L, D_MODEL, PEER_HEADS * PEER_DK), D_MODEL ** -0.5),
        "sub_keys1": nrm(ks[20], (L, PEER_HEADS, N_KEYS, PEER_DH), PEER_DH ** -0.5),
        "sub_keys2": nrm(ks[21], (L, PEER_HEADS, N_KEYS, PEER_DH), PEER_DH ** -0.5),
        "expert_u": nrm(ks[22], (L, N_EXPERTS, D_MODEL), D_MODEL ** -0.5),
        "expert_v": nrm(ks[23], (L, N_EXPERTS, D_MODEL), PEER_HEADS ** -0.5),
    }


def reference(x, c, w_mod, b_mod, g_norm1, w_in, g_q, g_k, conv_w, conv_b,
              w_rg, b_rg, w_ig, b_ig, lru_lambda, g_out_attn, g_out_lru, w_out,
              g_norm2, w_pq, sub_keys1, sub_keys2, expert_u, expert_v):
    f32 = jnp.float32
    out_dtype = x.dtype
    B, S, D = x.shape
    h_res = x.astype(f32)
    c_act = jax.nn.silu(c.astype(f32))
    for l in range(DEPTH):
        mod = c_act @ w_mod[l].astype(f32) + b_mod[l].astype(f32)
        shift1, scale1, gate1, shift2, scale2, gate2 = jnp.split(mod, 6, axis=-1)

        h = _rmsnorm(h_res, g_norm1[l]) * (1.0 + scale1[:, None]) + shift1[:, None]
        p = h @ w_in[l].astype(f32)
        q = p[..., 0:ATTN_W]
        k = p[..., ATTN_W:2 * ATTN_W]
        v = p[..., 2 * ATTN_W:3 * ATTN_W]
        lru_x = p[..., 3 * ATTN_W:3 * ATTN_W + LRU_W]
        lru_gate = p[..., 3 * ATTN_W + LRU_W:]

        def heads(t):
            return t.reshape(B, S, ATTN_HEADS, ATTN_HEAD_DIM).transpose(0, 2, 1, 3)

        qh = _rmsnorm(heads(q), g_q[l])
        kh = _rmsnorm(heads(k), g_k[l])
        attn = _stick_breaking_attention(qh, kh, heads(v))
        attn = attn.transpose(0, 2, 1, 3).reshape(B, S, ATTN_W)

        xb = _causal_depthwise_conv(lru_x, conv_w[l].astype(f32), conv_b[l].astype(f32))
        hr = _rg_lru(xb, w_rg[l].astype(f32), b_rg[l].astype(f32), w_ig[l].astype(f32),
                     b_ig[l].astype(f32), lru_lambda[l].astype(f32))
        rec = hr * jax.nn.gelu(lru_gate)

        mixed = jnp.concatenate([_rmsnorm(attn, g_out_attn[l]), _rmsnorm(rec, g_out_lru[l])], axis=-1)
        h_res = h_res + gate1[:, None] * (mixed @ w_out[l].astype(f32))

        h2 = _rmsnorm(h_res, g_norm2[l]) * (1.0 + scale2[:, None]) + shift2[:, None]
        y = _peer(h2, w_pq[l].astype(f32), sub_keys1[l].astype(f32), sub_keys2[l].astype(f32),
                  expert_u[l].astype(f32), expert_v[l].astype(f32))
        h_res = h_res + gate2[:, None] * y
    return h_res.astype(out_dtype)
```

```python
import functools
import math

import jax
import jax.numpy as jnp
from jax import lax
from jax.experimental import pallas as pl
from jax.experimental.pallas import tpu as pltpu

F32 = jnp.float32
BF16 = jnp.bfloat16
I32 = jnp.int32

EPS = 1e-6
ATTN_HEADS = 8
HEAD_DIM = 64
ATTN_W = ATTN_HEADS * HEAD_DIM
LRU_C = 8.0
CONV_W = 4
PEER_HEADS = 8
PEER_DK = 128
N_KEYS = 128
TOPK = 16

LANES = 128
SUBLANES = 8
VMEM_LIMIT = 56 * 1024 * 1024

ATTN_SKIP_LOG = -88.0
NEG_BIG = -1e30


def _cparams(*sem):
    return pltpu.CompilerParams(dimension_semantics=sem, vmem_limit_bytes=VMEM_LIMIT)


def _gelu_tanh(x):
    return 0.5 * x * (1.0 + jnp.tanh(math.sqrt(2.0 / math.pi) * (x + 0.044715 * x * x * x)))


def _softplus(x):
    return jnp.maximum(x, 0.0) + jnp.log1p(jnp.exp(-jnp.abs(x)))


def _sigmoid(x):
    return 1.0 / (1.0 + jnp.exp(-x))


def _split_bf16(v):
    hi = v.astype(BF16)
    lo = (v - hi.astype(F32)).astype(BF16)
    return hi, lo


def _mod_kernel(c_ref, w_ref, b_ref, o_ref):
    c = c_ref[...]
    ca = c * _sigmoid(c)
    o_ref[...] = jnp.dot(ca, w_ref[...], preferred_element_type=F32) + b_ref[...]


def _mod(c, w_mod, b_mod):
    B, D = c.shape
    n_out = w_mod.shape[1]
    bn = 1024
    return pl.pallas_call(
        _mod_kernel,
        grid=(n_out // bn,),
        in_specs=[
            pl.BlockSpec((B, D), lambda j: (0, 0)),
            pl.BlockSpec((D, bn), lambda j: (0, j)),
            pl.BlockSpec((1, bn), lambda j: (0, j)),
        ],
        out_specs=pl.BlockSpec((B, bn), lambda j: (0, j)),
        out_shape=jax.ShapeDtypeStruct((B, n_out), F32),
        compiler_params=_cparams("arbitrary"),
        name="mod",
    )(c, w_mod, b_mod.reshape(1, n_out))


def _pair_rmsnorm(t, g):
    lane = lax.broadcasted_iota(I32, t.shape, 1)
    first = lane < HEAD_DIM
    sq = t * t
    m_lo = jnp.sum(jnp.where(first, sq, 0.0), axis=-1, keepdims=True)
    m_hi = jnp.sum(jnp.where(first, 0.0, sq), axis=-1, keepdims=True)
    ms = jnp.where(first, m_lo, m_hi) * (1.0 / HEAD_DIM)
    return t * lax.rsqrt(ms + EPS) * g


def _in_proj_kernel(x_ref, g1_ref, sc_ref, sh_ref, w_ref, gq_ref, gk_ref,
                    qkv_ref, lx_ref, lg_ref):
    xf = x_ref[0]
    ms = jnp.mean(xf * xf, axis=-1, keepdims=True)
    h = xf * lax.rsqrt(ms + EPS) * g1_ref[...]
    h = (h * (1.0 + sc_ref[0]) + sh_ref[0]).astype(BF16)
    for cidx in range(5):
        p = jnp.dot(h, w_ref[:, cidx * ATTN_W:(cidx + 1) * ATTN_W], preferred_element_type=F32)
        if cidx < 2:
            g_ref = gq_ref if cidx == 0 else gk_ref
            for j in range(ATTN_W // LANES):
                sl = slice(j * LANES, (j + 1) * LANES)
                t = _pair_rmsnorm(p[:, sl], g_ref[:, sl])
                qkv_ref[0, :, cidx * ATTN_W + j * LANES:cidx * ATTN_W + (j + 1) * LANES] = t.astype(BF16)
        elif cidx == 2:
            qkv_ref[0, :, 2 * ATTN_W:3 * ATTN_W] = p.astype(BF16)
        elif cidx == 3:
            lx_ref[0] = p
        else:
            lg_ref[0] = p


def _in_proj(x, g1, scale1, shift1, w_in_bf, gq_t, gk_t, ts):
    B, S, D = x.shape
    n_cols = w_in_bf.shape[1]
    lw = n_cols - 3 * ATTN_W
    assert lw == 2 * ATTN_W
    vec = lambda: pl.BlockSpec((1, 1, D), lambda b, i: (b, 0, 0))
    return pl.pallas_call(
        _in_proj_kernel,
        grid=(B, S // ts),
        in_specs=[
            pl.BlockSpec((1, ts, D), lambda b, i: (b, i, 0)),
            pl.BlockSpec((1, D), lambda b, i: (0, 0)),
            vec(), vec(),
            pl.BlockSpec((D, n_cols), lambda b, i: (0, 0)),
            pl.BlockSpec((1, ATTN_W), lambda b, i: (0, 0)),
            pl.BlockSpec((1, ATTN_W), lambda b, i: (0, 0)),
        ],
        out_specs=[
            pl.BlockSpec((1, ts, 3 * ATTN_W), lambda b, i: (b, i, 0)),
            pl.BlockSpec((1, ts, ATTN_W), lambda b, i: (b, i, 0)),
            pl.BlockSpec((1, ts, ATTN_W), lambda b, i: (b, i, 0)),
        ],
        out_shape=[
            jax.ShapeDtypeStruct((B, S, 3 * ATTN_W), BF16),
            jax.ShapeDtypeStruct((B, S, ATTN_W), F32),
            jax.ShapeDtypeStruct((B, S, ATTN_W), F32),
        ],
        compiler_params=_cparams("arbitrary", "arbitrary"),
        name="in_proj",
    )(x, g1, scale1, shift1, w_in_bf, gq_t, gk_t)


def _attn_kernel(q_ref, k_ref, v_ref, o_ref, *, tq, tk):
    i = pl.program_id(2)
    t0 = i * tq
    lane = lax.broadcasted_iota(I32, (tq, LANES), 1)
    first = lane < HEAD_DIM
    q = q_ref[0]
    rj = lax.broadcasted_iota(I32, (tk, tk), 0)
    cs = lax.broadcasted_iota(I32, (tk, tk), 1)
    suffix = jnp.where(rj > cs, 1.0, 0.0).astype(BF16)
    qpos = t0 + lax.broadcasted_iota(I32, (tq, tk), 0)
    kiota = lax.broadcasted_iota(I32, (tq, tk), 1)
    jd = t0 // tk

    def one_head(qh):
        def cond(st):
            j, _, _, go = st
            return jnp.logical_and(j >= 0, go > 0)

        def step(st):
            j, carry, acc, _ = st
            k0 = pl.multiple_of(j * tk, tk)
            kj = k_ref[0, pl.ds(k0, tk), :]
            vj = v_ref[0, pl.ds(k0, tk), :]
            z = lax.dot_general(qh, kj, (((1,), (1,)), ((), ())), preferred_element_type=F32)
            z = jnp.where(k0 + kiota < qpos, z, NEG_BIG)
            sp = _softplus(z)
            lf = -sp
            ls = z - sp
            lf_hi, lf_lo = _split_bf16(lf)
            later = (jnp.dot(lf_hi, suffix, preferred_element_type=F32)
                     + jnp.dot(lf_lo, suffix, preferred_element_type=F32))
            w = jnp.exp(ls + later + carry)
            acc = acc + jnp.dot(w.astype(BF16), vj, preferred_element_type=F32)
            carry = carry + jnp.sum(lf, axis=-1, keepdims=True)
            go = (jnp.max(carry) >= ATTN_SKIP_LOG).astype(I32)
            return j - 1, carry, acc, go

        st0 = (jd, jnp.zeros((tq, 1), F32), jnp.zeros((tq, LANES), F32), jnp.int32(1))
        return lax.while_loop(cond, step, st0)[2]

    zero = jnp.zeros_like(q)
    acc_a = one_head(jnp.where(first, q, zero))
    acc_b = one_head(jnp.where(first, zero, q))
    o_ref[0] = jnp.where(first, acc_a, acc_b)


def _attention(qkv, tq, tk):
    B, S, _ = qkv.shape
    n_pairs = ATTN_W // LANES
    kernel = functools.partial(_attn_kernel, tq=tq, tk=tk)
    return pl.pallas_call(
        kernel,
        grid=(B, n_pairs, S // tq),
        in_specs=[
            pl.BlockSpec((1, tq, LANES), lambda b, p, i: (b, i, p)),
            pl.BlockSpec((1, S, LANES), lambda b, p, i: (b, 0, n_pairs + p)),
            pl.BlockSpec((1, S, LANES), lambda b, p, i: (b, 0, 2 * n_pairs + p)),
        ],
        out_specs=pl.BlockSpec((1, tq, LANES), lambda b, p, i: (b, i, p)),
        out_shape=jax.ShapeDtypeStruct((B, S, ATTN_W), F32),
        compiler_params=_cparams("arbitrary", "arbitrary", "arbitrary"),
        name="attn",
    )(qkv, qkv, qkv)


def _shift_rows(v, d, fill):
    rows = lax.broadcasted_iota(I32, v.shape, 0)
    return jnp.where(rows >= d, pltpu.roll(v, d, axis=0), fill)


def _lru_kernel(lx_ref, lg_ref, cw_ref, cb_ref, wr_ref, br_ref, wi_ref, bi_ref, lam_ref,
                o_ref, xs_ref, h_ref, *, tc):
    halo = SUBLANES

    @pl.when(pl.program_id(1) == 0)
    def _():
        xs_ref[0:halo, :] = jnp.zeros((halo, ATTN_W), F32)
        h_ref[...] = jnp.zeros_like(h_ref)

    x = lx_ref[0]
    xs_ref[halo:halo + tc, :] = x
    xb = cb_ref[...] + x * cw_ref[CONV_W - 1:CONV_W, :]
    for tap in range(CONV_W - 1):
        d = CONV_W - 1 - tap
        xb = xb + xs_ref[halo - d:halo - d + tc, :] * cw_ref[tap:tap + 1, :]
    xs_ref[0:halo, :] = x[tc - halo:tc, :]

    xbb = xb.astype(BF16)
    r = _sigmoid(jnp.dot(xbb, wr_ref[...], preferred_element_type=F32) + br_ref[...])
    ig = _sigmoid(jnp.dot(xbb, wi_ref[...], preferred_element_type=F32) + bi_ref[...])
    log_a = -LRU_C * r * _softplus(-lam_ref[...])
    a = jnp.exp(log_a)
    u = jnp.sqrt(jnp.tanh(-log_a) * (1.0 + a * a)) * (ig * xb)

    d = 1
    while d < tc:
        a_prev = _shift_rows(a, d, 1.0)
        u_prev = _shift_rows(u, d, 0.0)
        u = a * u_prev + u
        a = a * a_prev
        d *= 2
    h = a * h_ref[...] + u
    h_ref[...] = h[tc - 1:tc, :]
    o_ref[0] = h * _gelu_tanh(lg_ref[0])


def _lru(lx, lg, conv_w, conv_b, wr_bd, b_rg, wi_bd, b_ig, lam, tc):
    B, S, W = lx.shape
    row = lambda: pl.BlockSpec((1, W), lambda b, i: (0, 0))
    kernel = functools.partial(_lru_kernel, tc=tc)
    return pl.pallas_call(
        kernel,
        grid=(B, S // tc),
        in_specs=[
            pl.BlockSpec((1, tc, W), lambda b, i: (b, i, 0)),
            pl.BlockSpec((1, tc, W), lambda b, i: (b, i, 0)),
            pl.BlockSpec((CONV_W, W), lambda b, i: (0, 0)),
            row(),
            pl.BlockSpec((W, W), lambda b, i: (0, 0)),
            row(),
            pl.BlockSpec((W, W), lambda b, i: (0, 0)),
            row(), row(),
        ],
        out_specs=pl.BlockSpec((1, tc, W), lambda b, i: (b, i, 0)),
        out_shape=jax.ShapeDtypeStruct((B, S, W), F32),
        scratch_shapes=[pltpu.VMEM((tc + SUBLANES, W), F32), pltpu.VMEM((1, W), F32)],
        compiler_params=_cparams("arbitrary", "arbitrary"),
        name="lru",
    )(lx, lg, conv_w, conv_b, wr_bd, b_rg, wi_bd, b_ig, lam)


def _rms(v, g):
    ms = jnp.mean(v * v, axis=-1, keepdims=True)
    return v * lax.rsqrt(ms + EPS) * g


def _out_proj_kernel(at_ref, rc_ref, x_ref, ga_ref, gr_ref, wo_ref, g1_ref, g2_ref,
                     sc_ref, sh_ref, wq_ref, x1_ref, h2_ref, qp_ref):
    ma = _rms(at_ref[0], ga_ref[...]).astype(BF16)
    mr = _rms(rc_ref[0], gr_ref[...]).astype(BF16)
    y = (jnp.dot(ma, wo_ref[0:ATTN_W, :], preferred_element_type=F32)
         + jnp.dot(mr, wo_ref[ATTN_W:2 * ATTN_W, :], preferred_element_type=F32))
    x1 = x_ref[0] + g1_ref[0] * y
    x1_ref[0] = x1
    h2 = _rms(x1, g2_ref[...]) * (1.0 + sc_ref[0]) + sh_ref[0]
    h2_ref[0] = h2
    qp_ref[0] = jnp.dot(h2.astype(BF16), wq_ref[...], preferred_element_type=F32).astype(BF16)


def _out_proj(attn, rec, x, g_oa, g_ol, w_out_bf, gate1, g2, scale2, shift2, w_pq_bf, ts):
    B, S, D = x.shape
    W = attn.shape[-1]
    tok = lambda w: pl.BlockSpec((1, ts, w), lambda b, i: (b, i, 0))
    vec = lambda: pl.BlockSpec((1, 1, D), lambda b, i: (b, 0, 0))
    full = lambda r, c: pl.BlockSpec((r, c), lambda b, i: (0, 0))
    return pl.pallas_call(
        _out_proj_kernel,
        grid=(B, S // ts),
        in_specs=[tok(W), tok(W), tok(D), full(1, W), full(1, W), full(D, D), vec(),
                  full(1, D), vec(), vec(), full(D, D)],
        out_specs=[tok(D), tok(D), tok(D)],
        out_shape=[jax.ShapeDtypeStruct((B, S, D), F32),
                   jax.ShapeDtypeStruct((B, S, D), F32),
                   jax.ShapeDtypeStruct((B, S, D), BF16)],
        compiler_params=_cparams("arbitrary", "arbitrary"),
        name="out_proj",
    )(attn, rec, x, g_oa, g_ol, w_out_bf, gate1, g2, scale2, shift2, w_pq_bf)


def _extract_top(vals, ids, payload, rounds, id_limit):
    neg_inf = jnp.float32(-jnp.inf)
    lanes = vals.shape[1]
    slot = lax.broadcasted_iota(I32, (rounds, lanes), 0)
    out_v = jnp.zeros((rounds, lanes), F32)
    out_p = jnp.zeros((rounds, lanes), I32)
    for r in range(rounds):
        m = jnp.max(vals, axis=0, keepdims=True)
        sel = jnp.min(jnp.where(vals == m, ids, id_limit), axis=0, keepdims=True)
        hit = ids == sel
        taken = jnp.max(jnp.where(hit, payload, -1), axis=0, keepdims=True)
        out_v = jnp.where(slot == r, m, out_v)
        out_p = jnp.where(slot == r, taken, out_p)
        vals = jnp.where(hit, neg_inf, vals)
    return out_v, out_p


def _candidates(v1, v2, i1, i2):
    neg_inf = jnp.float32(-jnp.inf)
    row8 = lax.broadcasted_iota(I32, (SUBLANES, v1.shape[1]), 0)
    sc, pos, eid = [], [], []

    def add(v_rows, i_rows, a0, b, n_valid):
        s = v_rows + v2[b:b + 1, :]
        e = i_rows * N_KEYS + i2[b:b + 1, :]
        if n_valid < SUBLANES:
            s = jnp.where(row8 < n_valid, s, neg_inf)
        sc.append(s)
        pos.append((row8 + a0) * TOPK + b)
        eid.append(e)

    add(v1[0:8], i1[0:8], 0, 0, 8)
    add(v1[8:16], i1[8:16], 8, 0, 8)
    for b in range(1, 8):
        add(v1[0:8], i1[0:8], 0, b, TOPK // (b + 1))
    sc.append(v1[0:1, :] + v2[8:16])
    pos.append(row8 + 8)
    eid.append(i1[0:1, :] * N_KEYS + i2[8:16])
    return jnp.concatenate(sc, 0), jnp.concatenate(pos, 0), jnp.concatenate(eid, 0)


def _topk_kernel(qp_ref, kc_ref, idx_ref, g_ref, e_scr, g_scr, *, tt):
    kid = lax.broadcasted_iota(I32, (N_KEYS, tt), 0)

    def head(h, c):
        c0 = pl.multiple_of(h * PEER_DK, PEER_DK)
        qh = qp_ref[:, pl.ds(c0, PEER_DK)]
        st = lax.dot_general(kc_ref[h], qh, (((1,), (1,)), ((), ())),
                             preferred_element_type=F32)
        v1, i1 = _extract_top(st[0:N_KEYS], kid, kid, TOPK, N_KEYS)
        v2, i2 = _extract_top(st[N_KEYS:2 * N_KEYS], kid, kid, TOPK, N_KEYS)
        sc, pos, eid = _candidates(v1, v2, i1, i2)
        top_s, top_e = _extract_top(sc, pos, eid, TOPK, TOPK * TOPK)
        ex = jnp.exp(top_s - top_s[0:1, :])
        g = ex / jnp.sum(ex, axis=0, keepdims=True)
        r0 = pl.multiple_of(h * TOPK, TOPK)
        e_scr[pl.ds(r0, TOPK), :] = top_e * 4
        g_scr[pl.ds(r0, TOPK), :] = g
        return c

    lax.fori_loop(0, PEER_HEADS, head, 0)
    idx_ref[...] = e_scr[...].T
    g_ref[...] = g_scr[...].T


def _topk(qp2, kc, tt):
    N, D = qp2.shape
    kernel = functools.partial(_topk_kernel, tt=tt)
    slots = PEER_HEADS * TOPK
    return pl.pallas_call(
        kernel,
        grid=(N // tt,),
        in_specs=[
            pl.BlockSpec((tt, D), lambda i: (i, 0)),
            pl.BlockSpec(kc.shape, lambda i: (0, 0, 0)),
        ],
        out_specs=[pl.BlockSpec((tt, slots), lambda i: (i, 0)),
                   pl.BlockSpec((tt, slots), lambda i: (i, 0))],
        out_shape=[jax.ShapeDtypeStruct((N, slots), I32),
                   jax.ShapeDtypeStruct((N, slots), F32)],
        scratch_shapes=[pltpu.VMEM((slots, tt), I32), pltpu.VMEM((slots, tt), F32)],
        compiler_params=_cparams("arbitrary"),
        name="topk",
    )(qp2, kc)


ROWS = PEER_HEADS * TOPK
GROUPS = ROWS // SUBLANES
CHUNKS = 4


def _gather_rows(idx_ref, t, tab_ref, gbuf):
    base = t * ROWS
    for grp in range(GROUPS):
        for j in range(SUBLANES):
            e = idx_ref[base + grp * SUBLANES + j]
            gbuf[grp, pl.ds(j, CHUNKS, stride=SUBLANES), :] = (
                tab_ref[pl.ds(pl.multiple_of(e, CHUNKS), CHUNKS), :])


def _unpack(w):
    lo = pltpu.bitcast(w << 16, F32)
    hi = pltpu.bitcast(w & jnp.int32(-65536), F32)
    return lo, hi


def _peer_u_kernel(idx_ref, x_ref, g_ref, tab_ref, w_ref, gbuf0, gbuf1, q_scr, *, tb):
    def dots(t, gbuf):
        xrow = x_ref[pl.ds(t, 1), :]
        xb = [jnp.broadcast_to(xrow[:, c * LANES:(c + 1) * LANES], (SUBLANES, LANES))
              for c in range(2 * CHUNKS)]
        for grp in range(GROUPS):
            acc = None
            for c in range(CHUNKS):
                lo, hi = _unpack(gbuf[grp, pl.ds(c * SUBLANES, SUBLANES), :])
                p = lo * xb[c] + hi * xb[c + CHUNKS]
                acc = p if acc is None else acc + p
            q_scr[t, pl.ds(grp * SUBLANES, SUBLANES), :] = acc

    _gather_rows(idx_ref, 0, tab_ref, gbuf0)

    def pair(i, c):
        t0 = 2 * i
        _gather_rows(idx_ref, t0 + 1, tab_ref, gbuf1)
        dots(t0, gbuf0)
        _gather_rows(idx_ref, jnp.minimum(t0 + 2, tb - 1), tab_ref, gbuf0)
        dots(t0 + 1, gbuf1)
        return c

    lax.fori_loop(0, tb // 2, pair, 0)

    def finish(i, c):
        for k in range(SUBLANES):
            t = i * SUBLANES + k
            qt = q_scr[t].T
            s = jnp.sum(qt, axis=0, keepdims=True)
            w_ref[pl.ds(t, 1), :] = _gelu_tanh(s) * g_ref[pl.ds(t, 1), :]
        return c

    lax.fori_loop(0, tb // SUBLANES, finish, 0)


def _peer_u(idx_flat, h2, g, tab, tb):
    N, D = h2.shape
    kernel = functools.partial(_peer_u_kernel, tb=tb)
    return pl.pallas_call(
        kernel,
        grid=(N // tb,),
        in_specs=[
            pl.BlockSpec((tb * ROWS,), lambda i: (i,), memory_space=pltpu.SMEM),
            pl.BlockSpec((tb, D), lambda i: (i, 0)),
            pl.BlockSpec((tb, ROWS), lambda i: (i, 0)),
            pl.BlockSpec(memory_space=pltpu.VMEM),
        ],
        out_specs=pl.BlockSpec((tb, ROWS), lambda i: (i, 0)),
        out_shape=jax.ShapeDtypeStruct((N, ROWS), F32),
        scratch_shapes=[pltpu.VMEM((GROUPS, CHUNKS * SUBLANES, LANES), I32),
                        pltpu.VMEM((GROUPS, CHUNKS * SUBLANES, LANES), I32),
                        pltpu.VMEM((tb, ROWS, LANES), F32)],
        compiler_params=_cparams("arbitrary"),
        name="peer_u",
    )(idx_flat, h2, g, tab)


def _peer_v_kernel(idx_ref, w_ref, x1_ref, g2_ref, tab_ref, o_ref, gbuf0, gbuf1, *, tb):
    def combine(t, gbuf):
        wrow = w_ref[pl.ds(t, 1), :]
        wb = jnp.broadcast_to(wrow, (ROWS, ROWS)).T
        acc_lo = [None] * CHUNKS
        acc_hi = [None] * CHUNKS
        for grp in range(GROUPS):
            wg = wb[grp * SUBLANES:(grp + 1) * SUBLANES, :]
            for c in range(CHUNKS):
                lo, hi = _unpack(gbuf[grp, pl.ds(c * SUBLANES, SUBLANES), :])
                pl_, ph_ = lo * wg, hi * wg
                acc_lo[c] = pl_ if acc_lo[c] is None else acc_lo[c] + pl_
                acc_hi[c] = ph_ if acc_hi[c] is None else acc_hi[c] + ph_
        y = jnp.concatenate([jnp.sum(a, axis=0, keepdims=True) for a in acc_lo + acc_hi], axis=1)
        o_ref[pl.ds(t, 1), :] = x1_ref[pl.ds(t, 1), :] + g2_ref[0] * y

    _gather_rows(idx_ref, 0, tab_ref, gbuf0)

    def pair(i, c):
        t0 = 2 * i
        _gather_rows(idx_ref, t0 + 1, tab_ref, gbuf1)
        combine(t0, gbuf0)
        _gather_rows(idx_ref, jnp.minimum(t0 + 2, tb - 1), tab_ref, gbuf0)
        combine(t0 + 1, gbuf1)
        return c

    lax.fori_loop(0, tb // 2, pair, 0)


def _peer_v(idx_flat, w, x1, gate2, tab, tb, seq):
    N, D = x1.shape
    kernel = functools.partial(_peer_v_kernel, tb=tb)
    per_seq = seq // tb
    return pl.pallas_call(
        kernel,
        grid=(N // tb,),
        in_specs=[
            pl.BlockSpec((tb * ROWS,), lambda i: (i,), memory_space=pltpu.SMEM),
            pl.BlockSpec((tb, ROWS), lambda i: (i, 0)),
            pl.BlockSpec((tb, D), lambda i: (i, 0)),
            pl.BlockSpec((1, 1, D), lambda i: (i // per_seq, 0, 0)),
            pl.BlockSpec(memory_space=pltpu.VMEM),
        ],
        out_specs=pl.BlockSpec((tb, D), lambda i: (i, 0)),
        out_shape=jax.ShapeDtypeStruct((N, D), F32),
        scratch_shapes=[pltpu.VMEM((GROUPS, CHUNKS * SUBLANES, LANES), I32),
                        pltpu.VMEM((GROUPS, CHUNKS * SUBLANES, LANES), I32)],
        compiler_params=_cparams("arbitrary"),
        name="peer_v",
    )(idx_flat, w, x1, gate2, tab)


def _pack_table(tab):
    E, D = tab.shape
    half = D // 2
    bits = lax.bitcast_convert_type(tab.astype(BF16), jnp.uint16).astype(jnp.uint32)
    word = bits[:, :half] | (bits[:, half:] << 16)
    return lax.bitcast_convert_type(word, I32).reshape(E * (half // LANES), LANES)


def _block_diag(w):
    nb, n, _ = w.shape
    eye = jnp.eye(nb, dtype=w.dtype)
    return (eye[:, None, :, None] * w[:, :, None, :]).reshape(nb * n, nb * n)


def _key_matrix(k1, k2):
    z = jnp.zeros_like(k1)
    return jnp.concatenate([jnp.concatenate([k1, z], -1), jnp.concatenate([z, k2], -1)], axis=1)


def kernel(x, c, w_mod, b_mod, g_norm1, w_in, g_q, g_k, conv_w, conv_b, w_rg, b_rg, w_ig, b_ig,
           lru_lambda, g_out_attn, g_out_lru, w_out, g_norm2, w_pq, sub_keys1, sub_keys2,
           expert_u, expert_v):
    B, S, D = x.shape
    N = B * S
    depth = w_mod.shape[0]
    assert ATTN_W == 4 * LANES and D == 2 * ATTN_W
    ts = min(512, S)
    tq, tk = 128, 256
    tc = min(512, S)
    tt = 128
    tb = 64
    h_res = x.astype(F32)
    for l in range(depth):
        mod = _mod(c.astype(F32), w_mod[l].astype(F32), b_mod[l].astype(F32))
        shift1, scale1, gate1, shift2, scale2, gate2 = [
            m.reshape(B, 1, D) for m in jnp.split(mod, 6, axis=-1)]

        scale = 1.0 / math.sqrt(HEAD_DIM)
        gq_t = jnp.tile(g_q[l].astype(F32) * scale, ATTN_HEADS).reshape(1, ATTN_W)
        gk_t = jnp.tile(g_k[l].astype(F32), ATTN_HEADS).reshape(1, ATTN_W)
        qkv, lx, lg = _in_proj(h_res, g_norm1[l].astype(F32).reshape(1, D), scale1, shift1,
                               w_in[l].astype(BF16), gq_t, gk_t, ts)

        attn = _attention(qkv, tq, tk)
        W = lx.shape[-1]
        rec = _lru(lx, lg, conv_w[l].astype(F32), conv_b[l].astype(F32).reshape(1, W),
                   _block_diag(w_rg[l]).astype(BF16), b_rg[l].astype(F32).reshape(1, W),
                   _block_diag(w_ig[l]).astype(BF16), b_ig[l].astype(F32).reshape(1, W),
                   lru_lambda[l].astype(F32).reshape(1, W), tc)

        x1, h2, qp = _out_proj(attn, rec, h_res,
                               g_out_attn[l].astype(F32).reshape(1, W),
                               g_out_lru[l].astype(F32).reshape(1, W),
                               w_out[l].astype(BF16), gate1,
                               g_norm2[l].astype(F32).reshape(1, D), scale2, shift2,
                               w_pq[l].astype(BF16), ts)

        kc = _key_matrix(sub_keys1[l], sub_keys2[l]).astype(BF16)
        idx, g = _topk(qp.reshape(N, D), kc, tt)
        idx_flat = idx.reshape(N * ROWS)
        wts = _peer_u(idx_flat, h2.reshape(N, D), g, _pack_table(expert_u[l]), tb)
        out = _peer_v(idx_flat, wts, x1.reshape(N, D), gate2, _pack_table(expert_v[l]), tb, S)
        h_res = out.reshape(B, S, D)
    return h_res.astype(x.dtype)
```

```python
import functools
import math

import jax
import jax.numpy as jnp
from jax import lax
from jax.experimental import pallas as pl
from jax.experimental.pallas import tpu as pltpu

F32 = jnp.float32
BF16 = jnp.bfloat16
I32 = jnp.int32

EPS = 1e-6
ATTN_HEADS = 8
HEAD_DIM = 64
ATTN_W = ATTN_HEADS * HEAD_DIM
LRU_C = 8.0
CONV_W = 4
PEER_HEADS = 8
PEER_DK = 128
N_KEYS = 128
TOPK = 16

LANES = 128
SUBLANES = 8
VMEM_LIMIT = 56 * 1024 * 1024

ATTN_SKIP_LOG = -88.0
NEG_BIG = -1e30


def _cparams(*sem):
    return pltpu.CompilerParams(dimension_semantics=sem, vmem_limit_bytes=VMEM_LIMIT)


def _gelu_tanh(x):
    return 0.5 * x * (1.0 + jnp.tanh(math.sqrt(2.0 / math.pi) * (x + 0.044715 * x * x * x)))


def _softplus(x):
    return jnp.maximum(x, 0.0) + jnp.log1p(jnp.exp(-jnp.abs(x)))


def _sigmoid(x):
    return 1.0 / (1.0 + jnp.exp(-x))


def _split_bf16(v):
    hi = v.astype(BF16)
    lo = (v - hi.astype(F32)).astype(BF16)
    return hi, lo


def _mod_kernel(c_ref, w_ref, b_ref, o_ref):
    c = c_ref[...]
    ca = c * _sigmoid(c)
    o_ref[...] = jnp.dot(ca, w_ref[...], preferred_element_type=F32) + b_ref[...]


def _mod(c, w_mod, b_mod):
    B, D = c.shape
    n_out = w_mod.shape[1]
    bn = 1024
    return pl.pallas_call(
        _mod_kernel,
        grid=(n_out // bn,),
        in_specs=[
            pl.BlockSpec((B, D), lambda j: (0, 0)),
            pl.BlockSpec((D, bn), lambda j: (0, j)),
            pl.BlockSpec((1, bn), lambda j: (0, j)),
        ],
        out_specs=pl.BlockSpec((B, bn), lambda j: (0, j)),
        out_shape=jax.ShapeDtypeStruct((B, n_out), F32),
        compiler_params=_cparams("arbitrary"),
        name="mod",
    )(c, w_mod, b_mod.reshape(1, n_out))


def _pair_rmsnorm(t, g):
    lane = lax.broadcasted_iota(I32, t.shape, 1)
    first = lane < HEAD_DIM
    sq = t * t
    m_lo = jnp.sum(jnp.where(first, sq, 0.0), axis=-1, keepdims=True)
    m_hi = jnp.sum(jnp.where(first, 0.0, sq), axis=-1, keepdims=True)
    ms = jnp.where(first, m_lo, m_hi) * (1.0 / HEAD_DIM)
    return t * lax.rsqrt(ms + EPS) * g


def _in_proj_kernel(x_ref, g1_ref, sc_ref, sh_ref, w_ref, gq_ref, gk_ref,
                    qkv_ref, lx_ref, lg_ref):
    xf = x_ref[0]
    ms = jnp.mean(xf * xf, axis=-1, keepdims=True)
    h = xf * lax.rsqrt(ms + EPS) * g1_ref[...]
    h = (h * (1.0 + sc_ref[0]) + sh_ref[0]).astype(BF16)
    for cidx in range(5):
        p = jnp.dot(h, w_ref[:, cidx * ATTN_W:(cidx + 1) * ATTN_W], preferred_element_type=F32)
        if cidx < 2:
            g_ref = gq_ref if cidx == 0 else gk_ref
            for j in range(ATTN_W // LANES):
                sl = slice(j * LANES, (j + 1) * LANES)
                t = _pair_rmsnorm(p[:, sl], g_ref[:, sl])
                qkv_ref[0, :, cidx * ATTN_W + j * LANES:cidx * ATTN_W + (j + 1) * LANES] = t.astype(BF16)
        elif cidx == 2:
            qkv_ref[0, :, 2 * ATTN_W:3 * ATTN_W] = p.astype(BF16)
        elif cidx == 3:
            lx_ref[0] = p
        else:
            lg_ref[0] = p


def _in_proj(x, g1, scale1, shift1, w_in_bf, gq_t, gk_t, ts):
    B, S, D = x.shape
    n_cols = w_in_bf.shape[1]
    lw = n_cols - 3 * ATTN_W
    assert lw == 2 * ATTN_W
    vec = lambda: pl.BlockSpec((1, 1, D), lambda b, i: (b, 0, 0))
    return pl.pallas_call(
        _in_proj_kernel,
        grid=(B, S // ts),
        in_specs=[
            pl.BlockSpec((1, ts, D), lambda b, i: (b, i, 0)),
            pl.BlockSpec((1, D), lambda b, i: (0, 0)),
            vec(), vec(),
            pl.BlockSpec((D, n_cols), lambda b, i: (0, 0)),
            pl.BlockSpec((1, ATTN_W), lambda b, i: (0, 0)),
            pl.BlockSpec((1, ATTN_W), lambda b, i: (0, 0)),
        ],
        out_specs=[
            pl.BlockSpec((1, ts, 3 * ATTN_W), lambda b, i: (b, i, 0)),
            pl.BlockSpec((1, ts, ATTN_W), lambda b, i: (b, i, 0)),
            pl.BlockSpec((1, ts, ATTN_W), lambda b, i: (b, i, 0)),
        ],
        out_shape=[
            jax.ShapeDtypeStruct((B, S, 3 * ATTN_W), BF16),
            jax.ShapeDtypeStruct((B, S, ATTN_W), F32),
            jax.ShapeDtypeStruct((B, S, ATTN_W), F32),
        ],
        compiler_params=_cparams("arbitrary", "arbitrary"),
        name="in_proj",
    )(x, g1, scale1, shift1, w_in_bf, gq_t, gk_t)


def _attn_kernel(q_ref, k_ref, v_ref, o_ref, *, tq, tk):
    i = pl.program_id(2)
    t0 = i * tq
    lane = lax.broadcasted_iota(I32, (tq, LANES), 1)
    first = lane < HEAD_DIM
    q = q_ref[0]
    zero = jnp.zeros_like(q)
    qs = jnp.concatenate([jnp.where(first, q, zero), jnp.where(first, zero, q)], axis=0)
    rj = lax.broadcasted_iota(I32, (tk, tk), 0)
    cs = lax.broadcasted_iota(I32, (tk, tk), 1)
    suffix = jnp.where(rj > cs, 1.0, 0.0).astype(BF16)
    qpos = t0 + (lax.broadcasted_iota(I32, (2 * tq, tk), 0) & (tq - 1))
    kiota = lax.broadcasted_iota(I32, (2 * tq, tk), 1)

    def cond(st):
        end, go, _, _ = st
        return jnp.logical_and(end > 0, go > 0)

    def step(st):
        end, _, carry, acc = st
        k0 = pl.multiple_of(jnp.maximum(end - tk, 0), tq)
        kj = k_ref[0, pl.ds(k0, tk), :]
        vj = v_ref[0, pl.ds(k0, tk), :]
        kpos = k0 + kiota
        valid = jnp.logical_and(kpos < qpos, kpos < end)
        z = lax.dot_general(qs, kj, (((1,), (1,)), ((), ())), preferred_element_type=F32)
        z = jnp.where(valid, z, NEG_BIG)
        sp = _softplus(z)
        lf = -sp
        ls = z - sp
        lf_hi, lf_lo = _split_bf16(lf)
        later = (jnp.dot(lf_hi, suffix, preferred_element_type=F32)
                 + jnp.dot(lf_lo, suffix, preferred_element_type=F32))
        w = jnp.exp(ls + later + carry)
        acc = acc + jnp.dot(w.astype(BF16), vj, preferred_element_type=F32)
        carry = carry + jnp.sum(lf, axis=-1, keepdims=True)
        go = (jnp.max(carry) >= ATTN_SKIP_LOG).astype(I32)
        return end - tk, go, carry, acc

    st0 = (t0 + tq, jnp.int32(1), jnp.zeros((2 * tq, 1), F32), jnp.zeros((2 * tq, LANES), F32))
    acc = lax.while_loop(cond, step, st0)[3]
    o_ref[0] = jnp.where(first, acc[0:tq], acc[tq:2 * tq])


def _attention(qkv, tq, tk):
    B, S, _ = qkv.shape
    n_pairs = ATTN_W // LANES
    kernel = functools.partial(_attn_kernel, tq=tq, tk=tk)
    return pl.pallas_call(
        kernel,
        grid=(B, n_pairs, S // tq),
        in_specs=[
            pl.BlockSpec((1, tq, LANES), lambda b, p, i: (b, i, p)),
            pl.BlockSpec((1, S, LANES), lambda b, p, i: (b, 0, n_pairs + p)),
            pl.BlockSpec((1, S, LANES), lambda b, p, i: (b, 0, 2 * n_pairs + p)),
        ],
        out_specs=pl.BlockSpec((1, tq, LANES), lambda b, p, i: (b, i, p)),
        out_shape=jax.ShapeDtypeStruct((B, S, ATTN_W), F32),
        compiler_params=_cparams("arbitrary", "arbitrary", "arbitrary"),
        name="attn",
    )(qkv, qkv, qkv)


def _shift_rows(v, d, fill):
    rows = lax.broadcasted_iota(I32, v.shape, 0)
    return jnp.where(rows >= d, pltpu.roll(v, d, axis=0), fill)


def _lru_kernel(lx_ref, lg_ref, cw_ref, cb_ref, wr_ref, br_ref, wi_ref, bi_ref, lam_ref,
                o_ref, xs_ref, h_ref, *, tc):
    halo = SUBLANES

    @pl.when(pl.program_id(1) == 0)
    def _():
        xs_ref[0:halo, :] = jnp.zeros((halo, ATTN_W), F32)
        h_ref[...] = jnp.zeros_like(h_ref)

    x = lx_ref[0]
    xs_ref[halo:halo + tc, :] = x
    xb = cb_ref[...] + x * cw_ref[CONV_W - 1:CONV_W, :]
    for tap in range(CONV_W - 1):
        d = CONV_W - 1 - tap
        xb = xb + xs_ref[halo - d:halo - d + tc, :] * cw_ref[tap:tap + 1, :]
    xs_ref[0:halo, :] = x[tc - halo:tc, :]

    xbb = xb.astype(BF16)
    r = _sigmoid(jnp.dot(xbb, wr_ref[...], preferred_element_type=F32) + br_ref[...])
    ig = _sigmoid(jnp.dot(xbb, wi_ref[...], preferred_element_type=F32) + bi_ref[...])
    log_a = -LRU_C * r * _softplus(-lam_ref[...])
    a = jnp.exp(log_a)
    u = jnp.sqrt(jnp.tanh(-log_a) * (1.0 + a * a)) * (ig * xb)

    d = 1
    while d < tc:
        a_prev = _shift_rows(a, d, 1.0)
        u_prev = _shift_rows(u, d, 0.0)
        u = a * u_prev + u
        a = a * a_prev
        d *= 2
    h = a * h_ref[...] + u
    h_ref[...] = h[tc - 1:tc, :]
    o_ref[0] = h * _gelu_tanh(lg_ref[0])


def _lru(lx, lg, conv_w, conv_b, wr_bd, b_rg, wi_bd, b_ig, lam, tc):
    B, S, W = lx.shape
    row = lambda: pl.BlockSpec((1, W), lambda b, i: (0, 0))
    kernel = functools.partial(_lru_kernel, tc=tc)
    return pl.pallas_call(
        kernel,
        grid=(B, S // tc),
        in_specs=[
            pl.BlockSpec((1, tc, W), lambda b, i: (b, i, 0)),
            pl.BlockSpec((1, tc, W), lambda b, i: (b, i, 0)),
            pl.BlockSpec((CONV_W, W), lambda b, i: (0, 0)),
            row(),
            pl.BlockSpec((W, W), lambda b, i: (0, 0)),
            row(),
            pl.BlockSpec((W, W), lambda b, i: (0, 0)),
            row(), row(),
        ],
        out_specs=pl.BlockSpec((1, tc, W), lambda b, i: (b, i, 0)),
        out_shape=jax.ShapeDtypeStruct((B, S, W), F32),
        scratch_shapes=[pltpu.VMEM((tc + SUBLANES, W), F32), pltpu.VMEM((1, W), F32)],
        compiler_params=_cparams("arbitrary", "arbitrary"),
        name="lru",
    )(lx, lg, conv_w, conv_b, wr_bd, b_rg, wi_bd, b_ig, lam)


def _rms(v, g):
    ms = jnp.mean(v * v, axis=-1, keepdims=True)
    return v * lax.rsqrt(ms + EPS) * g


def _out_proj_kernel(at_ref, rc_ref, x_ref, ga_ref, gr_ref, wo_ref, g1_ref, g2_ref,
                     sc_ref, sh_ref, wq_ref, x1_ref, h2_ref, qp_ref):
    ma = _rms(at_ref[0], ga_ref[...]).astype(BF16)
    mr = _rms(rc_ref[0], gr_ref[...]).astype(BF16)
    y = (jnp.dot(ma, wo_ref[0:ATTN_W, :], preferred_element_type=F32)
         + jnp.dot(mr, wo_ref[ATTN_W:2 * ATTN_W, :], preferred_element_type=F32))
    x1 = x_ref[0] + g1_ref[0] * y
    x1_ref[0] = x1
    h2 = _rms(x1, g2_ref[...]) * (1.0 + sc_ref[0]) + sh_ref[0]
    h2_ref[0] = h2
    qp_ref[0] = jnp.dot(h2.astype(BF16), wq_ref[...], preferred_element_type=F32).astype(BF16)


def _out_proj(attn, rec, x, g_oa, g_ol, w_out_bf, gate1, g2, scale2, shift2, w_pq_bf, ts):
    B, S, D = x.shape
    W = attn.shape[-1]
    tok = lambda w: pl.BlockSpec((1, ts, w), lambda b, i: (b, i, 0))
    vec = lambda: pl.BlockSpec((1, 1, D), lambda b, i: (b, 0, 0))
    full = lambda r, c: pl.BlockSpec((r, c), lambda b, i: (0, 0))
    return pl.pallas_call(
        _out_proj_kernel,
        grid=(B, S // ts),
        in_specs=[tok(W), tok(W), tok(D), full(1, W), full(1, W), full(D, D), vec(),
                  full(1, D), vec(), vec(), full(D, D)],
        out_specs=[tok(D), tok(D), tok(D)],
        out_shape=[jax.ShapeDtypeStruct((B, S, D), F32),
                   jax.ShapeDtypeStruct((B, S, D), F32),
                   jax.ShapeDtypeStruct((B, S, D), BF16)],
        compiler_params=_cparams("arbitrary", "arbitrary"),
        name="out_proj",
    )(attn, rec, x, g_oa, g_ol, w_out_bf, gate1, g2, scale2, shift2, w_pq_bf)


def _tree(xs, op):
    xs = list(xs)
    while len(xs) > 1:
        nxt = [op(xs[k], xs[k + 1]) for k in range(0, len(xs) - 1, 2)]
        if len(xs) % 2:
            nxt.append(xs[-1])
        xs = nxt
    return xs[0]


def _all_sublanes(x, op):
    for s in (4, 2, 1):
        x = op(x, pltpu.roll(x, s, axis=0))
    return x


def _extract_top(problems, ids, rounds, id_limit):
    neg_inf = jnp.float32(-jnp.inf)
    shape = ids[0].shape
    row8 = lax.broadcasted_iota(I32, shape, 0)
    vals = [list(p[0]) for p in problems]
    out_v = [[jnp.zeros(shape, F32) for _ in range(rounds // SUBLANES)] for _ in problems]
    out_p = [[jnp.zeros(shape, I32) for _ in range(rounds // SUBLANES)] for _ in problems]
    for r in range(rounds):
        here = row8 == (r % SUBLANES)
        for n, (_, payload) in enumerate(problems):
            m = _all_sublanes(_tree(vals[n], jnp.maximum), jnp.maximum)
            first = [jnp.where(v == m, i, id_limit) for v, i in zip(vals[n], ids)]
            sel = _all_sublanes(_tree(first, jnp.minimum), jnp.minimum)
            hit = [i == sel for i in ids]
            if payload is None:
                taken = sel
            else:
                taken = _all_sublanes(
                    _tree([jnp.where(h, p, -1) for h, p in zip(hit, payload)], jnp.maximum),
                    jnp.maximum)
            out_v[n][r // SUBLANES] = jnp.where(here, m, out_v[n][r // SUBLANES])
            out_p[n][r // SUBLANES] = jnp.where(here, taken, out_p[n][r // SUBLANES])
            vals[n] = [jnp.where(h, neg_inf, v) for h, v in zip(hit, vals[n])]
    return [(out_v[n], out_p[n]) for n in range(len(problems))]


def _row(x, r):
    return jnp.broadcast_to(x[r:r + 1, :], x.shape)


def _candidates(v1, v2, i1, i2):
    neg_inf = jnp.float32(-jnp.inf)
    row8 = lax.broadcasted_iota(I32, v1[0].shape, 0)
    sc, pos, eid = [], [], []

    def add(v_rows, i_rows, a0, b, n_valid):
        s = v_rows + _row(v2[b // SUBLANES], b % SUBLANES)
        if n_valid < SUBLANES:
            s = jnp.where(row8 < n_valid, s, neg_inf)
        sc.append(s)
        pos.append((row8 + a0) * TOPK + b)
        eid.append(i_rows * N_KEYS + _row(i2[b // SUBLANES], b % SUBLANES))

    add(v1[0], i1[0], 0, 0, 8)
    add(v1[1], i1[1], 8, 0, 8)
    for b in range(1, 8):
        add(v1[0], i1[0], 0, b, TOPK // (b + 1))
    sc.append(_row(v1[0], 0) + v2[1])
    pos.append(row8 + 8)
    eid.append(_row(i1[0], 0) * N_KEYS + i2[1])
    return sc, pos, eid


def _topk_kernel(qp_ref, kc_ref, idx_ref, g_ref, e_scr, g_scr, *, tt):
    row8 = lax.broadcasted_iota(I32, (SUBLANES, tt), 0)
    kid = [row8 + k * SUBLANES for k in range(N_KEYS // SUBLANES)]
    n_tiles = N_KEYS // SUBLANES

    def head_pair(hp, c):
        cands = []
        for k in range(2):
            h = 2 * hp + k
            c0 = pl.multiple_of(h * PEER_DK, PEER_DK)
            qh = qp_ref[:, pl.ds(c0, PEER_DK)]
            st = lax.dot_general(kc_ref[h], qh, (((1,), (1,)), ((), ())),
                                 preferred_element_type=F32)
            tiles = [st[j * SUBLANES:(j + 1) * SUBLANES, :] for j in range(2 * n_tiles)]
            (v1, i1), (v2, i2) = _extract_top(
                [(tiles[:n_tiles], None), (tiles[n_tiles:], None)], kid, TOPK, N_KEYS)
            cands.append(_candidates(v1, v2, i1, i2))
        pos = cands[0][1]
        tops = _extract_top([(cd[0], cd[2]) for cd in cands], pos, TOPK, TOPK * TOPK)
        for k, (top_s, top_e) in enumerate(tops):
            best = _row(top_s[0], 0)
            ex = [jnp.exp(t - best) for t in top_s]
            denom = _all_sublanes(ex[0] + ex[1], jnp.add)
            r0 = (2 * hp + k) * TOPK
            for j in range(TOPK // SUBLANES):
                rj = pl.multiple_of(r0 + j * SUBLANES, SUBLANES)
                e_scr[pl.ds(rj, SUBLANES), :] = top_e[j] * 4
                g_scr[pl.ds(rj, SUBLANES), :] = ex[j] / denom
        return c

    lax.fori_loop(0, PEER_HEADS // 2, head_pair, 0)
    idx_ref[...] = e_scr[...].T
    g_ref[...] = g_scr[...].T


def _topk(qp2, kc, tt):
    N, D = qp2.shape
    kernel = functools.partial(_topk_kernel, tt=tt)
    slots = PEER_HEADS * TOPK
    return pl.pallas_call(
        kernel,
        grid=(N // tt,),
        in_specs=[
            pl.BlockSpec((tt, D), lambda i: (i, 0)),
            pl.BlockSpec(kc.shape, lambda i: (0, 0, 0)),
        ],
        out_specs=[pl.BlockSpec((tt, slots), lambda i: (i, 0)),
                   pl.BlockSpec((tt, slots), lambda i: (i, 0))],
        out_shape=[jax.ShapeDtypeStruct((N, slots), I32),
                   jax.ShapeDtypeStruct((N, slots), F32)],
        scratch_shapes=[pltpu.VMEM((slots, tt), I32), pltpu.VMEM((slots, tt), F32)],
        compiler_params=_cparams("arbitrary"),
        name="topk",
    )(qp2, kc)


ROWS = PEER_HEADS * TOPK
GROUPS = ROWS // SUBLANES
CHUNKS = 4


def _gather_rows(idx_ref, t, tab_ref, gbuf):
    base = t * ROWS
    for grp in range(GROUPS):
        for j in range(SUBLANES):
            e = idx_ref[base + grp * SUBLANES + j]
            gbuf[grp, pl.ds(j, CHUNKS, stride=SUBLANES), :] = (
                tab_ref[pl.ds(pl.multiple_of(e, CHUNKS), CHUNKS), :])


def _unpack(w):
    lo = pltpu.bitcast(w << 16, F32)
    hi = pltpu.bitcast(w & jnp.int32(-65536), F32)
    return lo, hi


def _peer_u_kernel(idx_ref, x_ref, g_ref, tab_ref, w_ref, gbuf0, gbuf1, q_scr, *, tb):
    ones = jnp.ones((SUBLANES, LANES), BF16)
    lane_dims = (((1,), (1,)), ((), ()))

    def dots(t, gbuf, slot):
        xrow = x_ref[pl.ds(t, 1), :]
        xb = [jnp.broadcast_to(xrow[:, c * LANES:(c + 1) * LANES], (SUBLANES, LANES))
              for c in range(2 * CHUNKS)]
        for grp in range(GROUPS):
            acc = None
            for c in range(CHUNKS):
                lo, hi = _unpack(gbuf[grp, pl.ds(c * SUBLANES, SUBLANES), :])
                p = lo * xb[c] + hi * xb[c + CHUNKS]
                acc = p if acc is None else acc + p
            q_scr[slot, pl.ds(grp * SUBLANES, SUBLANES), :] = acc

    def lane_sums(slot):
        qh, ql = _split_bf16(q_scr[slot])
        s = (lax.dot_general(ones, qh, lane_dims, preferred_element_type=F32)
             + lax.dot_general(ones, ql, lane_dims, preferred_element_type=F32))
        return s[0:1, :]

    def emit(t, s):
        w_ref[pl.ds(t, 1), :] = _gelu_tanh(s) * g_ref[pl.ds(t, 1), :]

    q_scr[...] = jnp.zeros_like(q_scr)
    _gather_rows(idx_ref, 0, tab_ref, gbuf0)

    def pair(i, c):
        t0 = 2 * i
        tp = jnp.maximum(t0 - 2, 0)
        s0 = lane_sums(0)
        s1 = lane_sums(1)
        _gather_rows(idx_ref, t0 + 1, tab_ref, gbuf1)
        dots(t0, gbuf0, 0)
        _gather_rows(idx_ref, jnp.minimum(t0 + 2, tb - 1), tab_ref, gbuf0)
        dots(t0 + 1, gbuf1, 1)
        emit(tp, s0)
        emit(tp + 1, s1)
        return c

    lax.fori_loop(0, tb // 2, pair, 0)
    emit(tb - 2, lane_sums(0))
    emit(tb - 1, lane_sums(1))


def _peer_u(idx_flat, h2, g, tab, tb):
    N, D = h2.shape
    kernel = functools.partial(_peer_u_kernel, tb=tb)
    return pl.pallas_call(
        kernel,
        grid=(N // tb,),
        in_specs=[
            pl.BlockSpec((tb * ROWS,), lambda i: (i,), memory_space=pltpu.SMEM),
            pl.BlockSpec((tb, D), lambda i: (i, 0)),
            pl.BlockSpec((tb, ROWS), lambda i: (i, 0)),
            pl.BlockSpec(memory_space=pltpu.VMEM),
        ],
        out_specs=pl.BlockSpec((tb, ROWS), lambda i: (i, 0)),
        out_shape=jax.ShapeDtypeStruct((N, ROWS), F32),
        scratch_shapes=[pltpu.VMEM((GROUPS, CHUNKS * SUBLANES, LANES), I32),
                        pltpu.VMEM((GROUPS, CHUNKS * SUBLANES, LANES), I32),
                        pltpu.VMEM((2, ROWS, LANES), F32)],
        compiler_params=_cparams("arbitrary"),
        name="peer_u",
    )(idx_flat, h2, g, tab)


def _peer_v_kernel(idx_ref, w_ref, x1_ref, g2_ref, tab_ref, o_ref, gbuf0, gbuf1, *, tb):
    def combine(t, gbuf):
        wrow = w_ref[pl.ds(t, 1), :]
        wb = jnp.broadcast_to(wrow, (ROWS, ROWS)).T
        acc_lo = [None] * CHUNKS
        acc_hi = [None] * CHUNKS
        for grp in range(GROUPS):
            wg = wb[grp * SUBLANES:(grp + 1) * SUBLANES, :]
            for c in range(CHUNKS):
                lo, hi = _unpack(gbuf[grp, pl.ds(c * SUBLANES, SUBLANES), :])
                pl_, ph_ = lo * wg, hi * wg
                acc_lo[c] = pl_ if acc_lo[c] is None else acc_lo[c] + pl_
                acc_hi[c] = ph_ if acc_hi[c] is None else acc_hi[c] + ph_
        y = jnp.concatenate([jnp.sum(a, axis=0, keepdims=True) for a in acc_lo + acc_hi], axis=1)
        o_ref[pl.ds(t, 1), :] = x1_ref[pl.ds(t, 1), :] + g2_ref[0] * y

    _gather_rows(idx_ref, 0, tab_ref, gbuf0)

    def pair(i, c):
        t0 = 2 * i
        _gather_rows(idx_ref, t0 + 1, tab_ref, gbuf1)
        combine(t0, gbuf0)
        _gather_rows(idx_ref, jnp.minimum(t0 + 2, tb - 1), tab_ref, gbuf0)
        combine(t0 + 1, gbuf1)
        return c

    lax.fori_loop(0, tb // 2, pair, 0)


def _peer_v(idx_flat, w, x1, gate2, tab, tb, seq):
    N, D = x1.shape
    kernel = functools.partial(_peer_v_kernel, tb=tb)
    per_seq = seq // tb
    return pl.pallas_call(
        kernel,
        grid=(N // tb,),
        in_specs=[
            pl.BlockSpec((tb * ROWS,), lambda i: (i,), memory_space=pltpu.SMEM),
            pl.BlockSpec((tb, ROWS), lambda i: (i, 0)),
            pl.BlockSpec((tb, D), lambda i: (i, 0)),
            pl.BlockSpec((1, 1, D), lambda i: (i // per_seq, 0, 0)),
            pl.BlockSpec(memory_space=pltpu.VMEM),
        ],
        out_specs=pl.BlockSpec((tb, D), lambda i: (i, 0)),
        out_shape=jax.ShapeDtypeStruct((N, D), F32),
        scratch_shapes=[pltpu.VMEM((GROUPS, CHUNKS * SUBLANES, LANES), I32),
                        pltpu.VMEM((GROUPS, CHUNKS * SUBLANES, LANES), I32)],
        compiler_params=_cparams("arbitrary"),
        name="peer_v",
    )(idx_flat, w, x1, gate2, tab)


def _pack_table(tab):
    E, D = tab.shape
    half = D // 2
    bits = lax.bitcast_convert_type(tab.astype(BF16), jnp.uint16).astype(jnp.uint32)
    word = bits[:, :half] | (bits[:, half:] << 16)
    return lax.bitcast_convert_type(word, I32).reshape(E * (half // LANES), LANES)


def _block_diag(w):
    nb, n, _ = w.shape
    eye = jnp.eye(nb, dtype=w.dtype)
    return (eye[:, None, :, None] * w[:, :, None, :]).reshape(nb * n, nb * n)


def _key_matrix(k1, k2):
    z = jnp.zeros_like(k1)
    return jnp.concatenate([jnp.concatenate([k1, z], -1), jnp.concatenate([z, k2], -1)], axis=1)


def kernel(x, c, w_mod, b_mod, g_norm1, w_in, g_q, g_k, conv_w, conv_b, w_rg, b_rg, w_ig, b_ig,
           lru_lambda, g_out_attn, g_out_lru, w_out, g_norm2, w_pq, sub_keys1, sub_keys2,
           expert_u, expert_v):
    B, S, D = x.shape
    N = B * S
    depth = w_mod.shape[0]
    assert ATTN_W == 4 * LANES and D == 2 * ATTN_W
    ts = min(512, S)
    tq, tk = 128, 256
    tc = min(512, S)
    tt = 128
    tb = 64
    h_res = x.astype(F32)
    for l in range(depth):
        mod = _mod(c.astype(F32), w_mod[l].astype(F32), b_mod[l].astype(F32))
        shift1, scale1, gate1, shift2, scale2, gate2 = [
            m.reshape(B, 1, D) for m in jnp.split(mod, 6, axis=-1)]

        scale = 1.0 / math.sqrt(HEAD_DIM)
        gq_t = jnp.tile(g_q[l].astype(F32) * scale, ATTN_HEADS).reshape(1, ATTN_W)
        gk_t = jnp.tile(g_k[l].astype(F32), ATTN_HEADS).reshape(1, ATTN_W)
        qkv, lx, lg = _in_proj(h_res, g_norm1[l].astype(F32).reshape(1, D), scale1, shift1,
                               w_in[l].astype(BF16), gq_t, gk_t, ts)

        attn = _attention(qkv, tq, tk)
        W = lx.shape[-1]
        rec = _lru(lx, lg, conv_w[l].astype(F32), conv_b[l].astype(F32).reshape(1, W),
                   _block_diag(w_rg[l]).astype(BF16), b_rg[l].astype(F32).reshape(1, W),
                   _block_diag(w_ig[l]).astype(BF16), b_ig[l].astype(F32).reshape(1, W),
                   lru_lambda[l].astype(F32).reshape(1, W), tc)

        x1, h2, qp = _out_proj(attn, rec, h_res,
                               g_out_attn[l].astype(F32).reshape(1, W),
                               g_out_lru[l].astype(F32).reshape(1, W),
                               w_out[l].astype(BF16), gate1,
                               g_norm2[l].astype(F32).reshape(1, D), scale2, shift2,
                               w_pq[l].astype(BF16), ts)

        kc = _key_matrix(sub_keys1[l], sub_keys2[l]).astype(BF16)
        idx, g = _topk(qp.reshape(N, D), kc, tt)
        idx_flat = idx.reshape(N * ROWS)
        wts = _peer_u(idx_flat, h2.reshape(N, D), g, _pack_table(expert_u[l]), tb)
        out = _peer_v(idx_flat, wts, x1.reshape(N, D), gate2, _pack_table(expert_v[l]), tb, S)
        h_res = out.reshape(B, S, D)
    return h_res.astype(x.dtype)
```

```python
import functools
import math

import jax
import jax.numpy as jnp
from jax import lax
from jax.experimental import pallas as pl
from jax.experimental.pallas import tpu as pltpu

F32 = jnp.float32
BF16 = jnp.bfloat16
I32 = jnp.int32

EPS = 1e-6
ATTN_HEADS = 8
HEAD_DIM = 64
ATTN_W = ATTN_HEADS * HEAD_DIM
LRU_C = 8.0
CONV_W = 4
PEER_HEADS = 8
PEER_DK = 128
N_KEYS = 128
TOPK = 16

LANES = 128
SUBLANES = 8
VMEM_LIMIT = 56 * 1024 * 1024

ATTN_SKIP_LOG = -88.0
ATTN_SUFFIX_BLOCK = 256
NEG_BIG = -1e30


def _cparams(*sem):
    return pltpu.CompilerParams(dimension_semantics=sem, vmem_limit_bytes=VMEM_LIMIT)


def _gelu_tanh(x):
    return 0.5 * x * (1.0 + jnp.tanh(math.sqrt(2.0 / math.pi) * (x + 0.044715 * x * x * x)))


def _softplus(x):
    return jnp.maximum(x, 0.0) + jnp.log1p(jnp.exp(-jnp.abs(x)))


def _sigmoid(x):
    return 1.0 / (1.0 + jnp.exp(-x))


def _split_bf16(v):
    hi = v.astype(BF16)
    lo = (v - hi.astype(F32)).astype(BF16)
    return hi, lo


def _mod_kernel(c_ref, w_ref, b_ref, o_ref):
    c = c_ref[...]
    ca = c * _sigmoid(c)
    o_ref[...] = jnp.dot(ca, w_ref[...], preferred_element_type=F32) + b_ref[...]


def _mod(c, w_mod, b_mod):
    B, D = c.shape
    n_out = w_mod.shape[1]
    bn = 1024
    return pl.pallas_call(
        _mod_kernel,
        grid=(n_out // bn,),
        in_specs=[
            pl.BlockSpec((B, D), lambda j: (0, 0)),
            pl.BlockSpec((D, bn), lambda j: (0, j)),
            pl.BlockSpec((1, bn), lambda j: (0, j)),
        ],
        out_specs=pl.BlockSpec((B, bn), lambda j: (0, j)),
        out_shape=jax.ShapeDtypeStruct((B, n_out), F32),
        compiler_params=_cparams("arbitrary"),
        name="mod",
    )(c, w_mod, b_mod.reshape(1, n_out))


def _pair_rmsnorm(t, g):
    lane = lax.broadcasted_iota(I32, t.shape, 1)
    first = lane < HEAD_DIM
    sq = t * t
    m_lo = jnp.sum(jnp.where(first, sq, 0.0), axis=-1, keepdims=True)
    m_hi = jnp.sum(jnp.where(first, 0.0, sq), axis=-1, keepdims=True)
    ms = jnp.where(first, m_lo, m_hi) * (1.0 / HEAD_DIM)
    return t * lax.rsqrt(ms + EPS) * g


def _in_proj_kernel(x_ref, g1_ref, sc_ref, sh_ref, w_ref, gq_ref, gk_ref,
                    qkv_ref, lx_ref, lg_ref):
    xf = x_ref[0]
    ms = jnp.mean(xf * xf, axis=-1, keepdims=True)
    h = xf * lax.rsqrt(ms + EPS) * g1_ref[...]
    h = (h * (1.0 + sc_ref[0]) + sh_ref[0]).astype(BF16)
    for cidx in range(5):
        p = jnp.dot(h, w_ref[:, cidx * ATTN_W:(cidx + 1) * ATTN_W], preferred_element_type=F32)
        if cidx < 2:
            g_ref = gq_ref if cidx == 0 else gk_ref
            for j in range(ATTN_W // LANES):
                sl = slice(j * LANES, (j + 1) * LANES)
                t = _pair_rmsnorm(p[:, sl], g_ref[:, sl])
                qkv_ref[0, :, cidx * ATTN_W + j * LANES:cidx * ATTN_W + (j + 1) * LANES] = t.astype(BF16)
        elif cidx == 2:
            qkv_ref[0, :, 2 * ATTN_W:3 * ATTN_W] = p.astype(BF16)
        elif cidx == 3:
            lx_ref[0] = p
        else:
            lg_ref[0] = p


def _in_proj(x, g1, scale1, shift1, w_in_bf, gq_t, gk_t, ts):
    B, S, D = x.shape
    n_cols = w_in_bf.shape[1]
    lw = n_cols - 3 * ATTN_W
    assert lw == 2 * ATTN_W
    vec = lambda: pl.BlockSpec((1, 1, D), lambda b, i: (b, 0, 0))
    return pl.pallas_call(
        _in_proj_kernel,
        grid=(B, S // ts),
        in_specs=[
            pl.BlockSpec((1, ts, D), lambda b, i: (b, i, 0)),
            pl.BlockSpec((1, D), lambda b, i: (0, 0)),
            vec(), vec(),
            pl.BlockSpec((D, n_cols), lambda b, i: (0, 0)),
            pl.BlockSpec((1, ATTN_W), lambda b, i: (0, 0)),
            pl.BlockSpec((1, ATTN_W), lambda b, i: (0, 0)),
        ],
        out_specs=[
            pl.BlockSpec((1, ts, 3 * ATTN_W), lambda b, i: (b, i, 0)),
            pl.BlockSpec((1, ts, ATTN_W), lambda b, i: (b, i, 0)),
            pl.BlockSpec((1, ts, ATTN_W), lambda b, i: (b, i, 0)),
        ],
        out_shape=[
            jax.ShapeDtypeStruct((B, S, 3 * ATTN_W), BF16),
            jax.ShapeDtypeStruct((B, S, ATTN_W), F32),
            jax.ShapeDtypeStruct((B, S, ATTN_W), F32),
        ],
        compiler_params=_cparams("arbitrary", "arbitrary"),
        name="in_proj",
    )(x, g1, scale1, shift1, w_in_bf, gq_t, gk_t)


def _attn_kernel(q_ref, k_ref, v_ref, o_ref, *, tq, tk):
    i = pl.program_id(2)
    t0 = i * tq
    lane = lax.broadcasted_iota(I32, (tq, LANES), 1)
    first = lane < HEAD_DIM
    q = q_ref[0]
    zero = jnp.zeros_like(q)
    qs = jnp.concatenate([jnp.where(first, q, zero), jnp.where(first, zero, q)], axis=0)
    sb = ATTN_SUFFIX_BLOCK
    rj = lax.broadcasted_iota(I32, (sb, sb), 0)
    cs = lax.broadcasted_iota(I32, (sb, sb), 1)
    suffix = jnp.where(rj > cs, 1.0, 0.0).astype(BF16)
    qpos = t0 + (lax.broadcasted_iota(I32, (2 * tq, tk), 0) & (tq - 1))
    kiota = lax.broadcasted_iota(I32, (2 * tq, tk), 1)

    def cond(st):
        end, go, _, _ = st
        return jnp.logical_and(end > 0, go > 0)

    def step(st):
        end, _, carry, acc = st
        k0 = pl.multiple_of(jnp.maximum(end - tk, 0), tq)
        kj = k_ref[0, pl.ds(k0, tk), :]
        vj = v_ref[0, pl.ds(k0, tk), :]
        kpos = k0 + kiota
        valid = jnp.logical_and(kpos < qpos, kpos < end)
        z = lax.dot_general(qs, kj, (((1,), (1,)), ((), ())), preferred_element_type=F32)
        z = jnp.where(valid, z, NEG_BIG)
        sp = _softplus(z)
        lf = -sp
        ls = z - sp
        after = carry
        parts = []
        for b in reversed(range(tk // sb)):
            lfb = lf[:, b * sb:(b + 1) * sb]
            lf_hi, lf_lo = _split_bf16(lfb)
            inside = (jnp.dot(lf_hi, suffix, preferred_element_type=F32)
                      + jnp.dot(lf_lo, suffix, preferred_element_type=F32))
            parts.append(inside + after)
            after = after + jnp.sum(lfb, axis=-1, keepdims=True)
        later = jnp.concatenate(parts[::-1], axis=1)
        w = jnp.exp(ls + later)
        acc = acc + jnp.dot(w.astype(BF16), vj, preferred_element_type=F32)
        carry = after
        go = (jnp.max(carry) >= ATTN_SKIP_LOG).astype(I32)
        return end - tk, go, carry, acc

    st0 = (t0 + tq, jnp.int32(1), jnp.zeros((2 * tq, 1), F32), jnp.zeros((2 * tq, LANES), F32))
    acc = lax.while_loop(cond, step, st0)[3]
    o_ref[0] = jnp.where(first, acc[0:tq], acc[tq:2 * tq])


def _attention(qkv, tq, tk):
    B, S, _ = qkv.shape
    n_pairs = ATTN_W // LANES
    kernel = functools.partial(_attn_kernel, tq=tq, tk=tk)
    return pl.pallas_call(
        kernel,
        grid=(B, n_pairs, S // tq),
        in_specs=[
            pl.BlockSpec((1, tq, LANES), lambda b, p, i: (b, i, p)),
            pl.BlockSpec((1, S, LANES), lambda b, p, i: (b, 0, n_pairs + p)),
            pl.BlockSpec((1, S, LANES), lambda b, p, i: (b, 0, 2 * n_pairs + p)),
        ],
        out_specs=pl.BlockSpec((1, tq, LANES), lambda b, p, i: (b, i, p)),
        out_shape=jax.ShapeDtypeStruct((B, S, ATTN_W), F32),
        compiler_params=_cparams("arbitrary", "arbitrary", "arbitrary"),
        name="attn",
    )(qkv, qkv, qkv)


def _shift_rows(v, d, fill):
    rows = lax.broadcasted_iota(I32, v.shape, 0)
    return jnp.where(rows >= d, pltpu.roll(v, d, axis=0), fill)


def _lru_kernel(lx_ref, lg_ref, cw_ref, cb_ref, wr_ref, br_ref, wi_ref, bi_ref, lam_ref,
                o_ref, xs_ref, h_ref, *, tc):
    halo = SUBLANES

    @pl.when(pl.program_id(1) == 0)
    def _():
        xs_ref[0:halo, :] = jnp.zeros((halo, ATTN_W), F32)
        h_ref[...] = jnp.zeros_like(h_ref)

    x = lx_ref[0]
    xs_ref[halo:halo + tc, :] = x
    xb = cb_ref[...] + x * cw_ref[CONV_W - 1:CONV_W, :]
    for tap in range(CONV_W - 1):
        d = CONV_W - 1 - tap
        xb = xb + xs_ref[halo - d:halo - d + tc, :] * cw_ref[tap:tap + 1, :]
    xs_ref[0:halo, :] = x[tc - halo:tc, :]

    xbb = xb.astype(BF16)
    r = _sigmoid(jnp.dot(xbb, wr_ref[...], preferred_element_type=F32) + br_ref[...])
    ig = _sigmoid(jnp.dot(xbb, wi_ref[...], preferred_element_type=F32) + bi_ref[...])
    log_a = -LRU_C * r * _softplus(-lam_ref[...])
    a = jnp.exp(log_a)
    u = jnp.sqrt(jnp.tanh(-log_a) * (1.0 + a * a)) * (ig * xb)

    d = 1
    while d < tc:
        a_prev = _shift_rows(a, d, 1.0)
        u_prev = _shift_rows(u, d, 0.0)
        u = a * u_prev + u
        a = a * a_prev
        d *= 2
    h = a * h_ref[...] + u
    h_ref[...] = h[tc - 1:tc, :]
    o_ref[0] = h * _gelu_tanh(lg_ref[0])


def _lru(lx, lg, conv_w, conv_b, wr_bd, b_rg, wi_bd, b_ig, lam, tc):
    B, S, W = lx.shape
    row = lambda: pl.BlockSpec((1, W), lambda b, i: (0, 0))
    kernel = functools.partial(_lru_kernel, tc=tc)
    return pl.pallas_call(
        kernel,
        grid=(B, S // tc),
        in_specs=[
            pl.BlockSpec((1, tc, W), lambda b, i: (b, i, 0)),
            pl.BlockSpec((1, tc, W), lambda b, i: (b, i, 0)),
            pl.BlockSpec((CONV_W, W), lambda b, i: (0, 0)),
            row(),
            pl.BlockSpec((W, W), lambda b, i: (0, 0)),
            row(),
            pl.BlockSpec((W, W), lambda b, i: (0, 0)),
            row(), row(),
        ],
        out_specs=pl.BlockSpec((1, tc, W), lambda b, i: (b, i, 0)),
        out_shape=jax.ShapeDtypeStruct((B, S, W), F32),
        scratch_shapes=[pltpu.VMEM((tc + SUBLANES, W), F32), pltpu.VMEM((1, W), F32)],
        compiler_params=_cparams("arbitrary", "arbitrary"),
        name="lru",
    )(lx, lg, conv_w, conv_b, wr_bd, b_rg, wi_bd, b_ig, lam)


def _rms(v, g):
    ms = jnp.mean(v * v, axis=-1, keepdims=True)
    return v * lax.rsqrt(ms + EPS) * g


def _out_proj_kernel(at_ref, rc_ref, x_ref, ga_ref, gr_ref, wo_ref, g1_ref, g2_ref,
                     sc_ref, sh_ref, wq_ref, x1_ref, h2_ref, qp_ref):
    ma = _rms(at_ref[0], ga_ref[...]).astype(BF16)
    mr = _rms(rc_ref[0], gr_ref[...]).astype(BF16)
    y = (jnp.dot(ma, wo_ref[0:ATTN_W, :], preferred_element_type=F32)
         + jnp.dot(mr, wo_ref[ATTN_W:2 * ATTN_W, :], preferred_element_type=F32))
    x1 = x_ref[0] + g1_ref[0] * y
    x1_ref[0] = x1
    h2 = _rms(x1, g2_ref[...]) * (1.0 + sc_ref[0]) + sh_ref[0]
    h2_ref[0] = h2
    qp_ref[0] = jnp.dot(h2.astype(BF16), wq_ref[...], preferred_element_type=F32).astype(BF16)


def _out_proj(attn, rec, x, g_oa, g_ol, w_out_bf, gate1, g2, scale2, shift2, w_pq_bf, ts):
    B, S, D = x.shape
    W = attn.shape[-1]
    tok = lambda w: pl.BlockSpec((1, ts, w), lambda b, i: (b, i, 0))
    vec = lambda: pl.BlockSpec((1, 1, D), lambda b, i: (b, 0, 0))
    full = lambda r, c: pl.BlockSpec((r, c), lambda b, i: (0, 0))
    return pl.pallas_call(
        _out_proj_kernel,
        grid=(B, S // ts),
        in_specs=[tok(W), tok(W), tok(D), full(1, W), full(1, W), full(D, D), vec(),
                  full(1, D), vec(), vec(), full(D, D)],
        out_specs=[tok(D), tok(D), tok(D)],
        out_shape=[jax.ShapeDtypeStruct((B, S, D), F32),
                   jax.ShapeDtypeStruct((B, S, D), F32),
                   jax.ShapeDtypeStruct((B, S, D), BF16)],
        compiler_params=_cparams("arbitrary", "arbitrary"),
        name="out_proj",
    )(attn, rec, x, g_oa, g_ol, w_out_bf, gate1, g2, scale2, shift2, w_pq_bf)


def _tree(xs, op):
    xs = list(xs)
    while len(xs) > 1:
        nxt = [op(xs[k], xs[k + 1]) for k in range(0, len(xs) - 1, 2)]
        if len(xs) % 2:
            nxt.append(xs[-1])
        xs = nxt
    return xs[0]


def _all_sublanes(x, op):
    for s in (4, 2, 1):
        x = op(x, pltpu.roll(x, s, axis=0))
    return x


def _extract_top(problems, ids, rounds, id_limit):
    neg_inf = jnp.float32(-jnp.inf)
    shape = ids[0].shape
    row8 = lax.broadcasted_iota(I32, shape, 0)
    vals = [list(p[0]) for p in problems]
    out_v = [[jnp.zeros(shape, F32) for _ in range(rounds // SUBLANES)] for _ in problems]
    out_p = [[jnp.zeros(shape, I32) for _ in range(rounds // SUBLANES)] for _ in problems]
    for r in range(rounds):
        here = row8 == (r % SUBLANES)
        for n, (_, payload) in enumerate(problems):
            m = _all_sublanes(_tree(vals[n], jnp.maximum), jnp.maximum)
            first = [jnp.where(v == m, i, id_limit) for v, i in zip(vals[n], ids)]
            sel = _all_sublanes(_tree(first, jnp.minimum), jnp.minimum)
            hit = [i == sel for i in ids]
            if payload is None:
                taken = sel
            else:
                taken = _all_sublanes(
                    _tree([jnp.where(h, p, -1) for h, p in zip(hit, payload)], jnp.maximum),
                    jnp.maximum)
            out_v[n][r // SUBLANES] = jnp.where(here, m, out_v[n][r // SUBLANES])
            out_p[n][r // SUBLANES] = jnp.where(here, taken, out_p[n][r // SUBLANES])
            vals[n] = [jnp.where(h, neg_inf, v) for h, v in zip(hit, vals[n])]
    return [(out_v[n], out_p[n]) for n in range(len(problems))]


def _sorting_network(n):
    pairs = []
    p = 1
    while p < n:
        k = p
        while k >= 1:
            for j in range(k % p, n - k, 2 * k):
                for i in range(min(k, n - j - k)):
                    if (i + j) // (2 * p) == (i + j + k) // (2 * p):
                        pairs.append((i + j, i + j + k))
            k //= 2
        p *= 2
    return pairs


def _top_of_keys(problems, ids, rounds):
    n = len(ids)
    shape = ids[0].shape
    row8 = lax.broadcasted_iota(I32, shape, 0)
    big = jnp.int32(n * SUBLANES)
    vals = [list(v) for v in problems]
    keys = [list(ids) for _ in problems]
    for i, j in _sorting_network(n):
        for v, k in zip(vals, keys):
            swap = (v[j] > v[i]) | ((v[j] == v[i]) & (k[j] < k[i]))
            v[i], v[j] = jnp.where(swap, v[j], v[i]), jnp.where(swap, v[i], v[j])
            k[i], k[j] = jnp.where(swap, k[j], k[i]), jnp.where(swap, k[i], k[j])
    out_v = [[jnp.zeros(shape, F32) for _ in range(rounds // SUBLANES)] for _ in problems]
    out_k = [[jnp.zeros(shape, I32) for _ in range(rounds // SUBLANES)] for _ in problems]
    for r in range(rounds):
        here = row8 == (r % SUBLANES)
        depth = min(rounds - r, n)
        for p, (v, k) in enumerate(zip(vals, keys)):
            m = _all_sublanes(v[0], jnp.maximum)
            sel = _all_sublanes(jnp.where(v[0] == m, k[0], big), jnp.minimum)
            win = k[0] == sel
            out_v[p][r // SUBLANES] = jnp.where(here, m, out_v[p][r // SUBLANES])
            out_k[p][r // SUBLANES] = jnp.where(here, sel, out_k[p][r // SUBLANES])
            for lvl in range(depth - 1):
                v[lvl] = jnp.where(win, v[lvl + 1], v[lvl])
                k[lvl] = jnp.where(win, k[lvl + 1], k[lvl])
    return [(out_v[p], out_k[p]) for p in range(len(problems))]


def _row(x, r):
    return jnp.broadcast_to(x[r:r + 1, :], x.shape)


def _candidates(v1, v2, i1, i2):
    neg_inf = jnp.float32(-jnp.inf)
    row8 = lax.broadcasted_iota(I32, v1[0].shape, 0)
    sc, pos, eid = [], [], []

    def add(v_rows, i_rows, a0, b, n_valid):
        s = v_rows + _row(v2[b // SUBLANES], b % SUBLANES)
        if n_valid < SUBLANES:
            s = jnp.where(row8 < n_valid, s, neg_inf)
        sc.append(s)
        pos.append((row8 + a0) * TOPK + b)
        eid.append(i_rows * N_KEYS + _row(i2[b // SUBLANES], b % SUBLANES))

    add(v1[0], i1[0], 0, 0, 8)
    add(v1[1], i1[1], 8, 0, 8)
    for b in range(1, 8):
        add(v1[0], i1[0], 0, b, TOPK // (b + 1))
    sc.append(_row(v1[0], 0) + v2[1])
    pos.append(row8 + 8)
    eid.append(_row(i1[0], 0) * N_KEYS + i2[1])
    return sc, pos, eid


def _topk_kernel(qp_ref, kc_ref, idx_ref, g_ref, e_scr, g_scr, *, tt):
    row8 = lax.broadcasted_iota(I32, (SUBLANES, tt), 0)
    kid = [row8 + k * SUBLANES for k in range(N_KEYS // SUBLANES)]
    n_tiles = N_KEYS // SUBLANES

    def head_pair(hp, c):
        cands = []
        for k in range(2):
            h = 2 * hp + k
            c0 = pl.multiple_of(h * PEER_DK, PEER_DK)
            qh = qp_ref[:, pl.ds(c0, PEER_DK)]
            st = lax.dot_general(kc_ref[h], qh, (((1,), (1,)), ((), ())),
                                 preferred_element_type=F32)
            tiles = [st[j * SUBLANES:(j + 1) * SUBLANES, :] for j in range(2 * n_tiles)]
            (v1, i1), (v2, i2) = _top_of_keys([tiles[:n_tiles], tiles[n_tiles:]], kid, TOPK)
            cands.append(_candidates(v1, v2, i1, i2))
        pos = cands[0][1]
        tops = _extract_top([(cd[0], cd[2]) for cd in cands], pos, TOPK, TOPK * TOPK)
        for k, (top_s, top_e) in enumerate(tops):
            best = _row(top_s[0], 0)
            ex = [jnp.exp(t - best) for t in top_s]
            denom = _all_sublanes(ex[0] + ex[1], jnp.add)
            r0 = (2 * hp + k) * TOPK
            for j in range(TOPK // SUBLANES):
                rj = pl.multiple_of(r0 + j * SUBLANES, SUBLANES)
                e_scr[pl.ds(rj, SUBLANES), :] = top_e[j] * 4
                g_scr[pl.ds(rj, SUBLANES), :] = ex[j] / denom
        return c

    lax.fori_loop(0, PEER_HEADS // 2, head_pair, 0)
    idx_ref[...] = e_scr[...].T
    g_ref[...] = g_scr[...].T


def _topk(qp2, kc, tt):
    N, D = qp2.shape
    kernel = functools.partial(_topk_kernel, tt=tt)
    slots = PEER_HEADS * TOPK
    return pl.pallas_call(
        kernel,
        grid=(N // tt,),
        in_specs=[
            pl.BlockSpec((tt, D), lambda i: (i, 0)),
            pl.BlockSpec(kc.shape, lambda i: (0, 0, 0)),
        ],
        out_specs=[pl.BlockSpec((tt, slots), lambda i: (i, 0)),
                   pl.BlockSpec((tt, slots), lambda i: (i, 0))],
        out_shape=[jax.ShapeDtypeStruct((N, slots), I32),
                   jax.ShapeDtypeStruct((N, slots), F32)],
        scratch_shapes=[pltpu.VMEM((slots, tt), I32), pltpu.VMEM((slots, tt), F32)],
        compiler_params=_cparams("arbitrary"),
        name="topk",
    )(qp2, kc)


ROWS = PEER_HEADS * TOPK
GROUPS = ROWS // SUBLANES
CHUNKS = 4


def _gather_rows(idx_ref, t, tab_ref, gbuf):
    base = t * ROWS
    for grp in range(GROUPS):
        for j in range(SUBLANES):
            e = idx_ref[base + grp * SUBLANES + j]
            gbuf[grp, pl.ds(j, CHUNKS, stride=SUBLANES), :] = (
                tab_ref[pl.ds(pl.multiple_of(e, CHUNKS), CHUNKS), :])


def _unpack(w):
    lo = pltpu.bitcast(w << 16, F32)
    hi = pltpu.bitcast(w & jnp.int32(-65536), F32)
    return lo, hi


def _peer_u_kernel(idx_ref, x_ref, g_ref, tab_ref, w_ref, gbuf0, gbuf1, q_scr, *, tb):
    ones = jnp.ones((SUBLANES, LANES), BF16)
    lane_dims = (((1,), (1,)), ((), ()))

    def dots(t, gbuf, slot):
        xrow = x_ref[pl.ds(t, 1), :]
        xb = [jnp.broadcast_to(xrow[:, c * LANES:(c + 1) * LANES], (SUBLANES, LANES))
              for c in range(2 * CHUNKS)]
        for grp in range(GROUPS):
            acc = None
            for c in range(CHUNKS):
                lo, hi = _unpack(gbuf[grp, pl.ds(c * SUBLANES, SUBLANES), :])
                p = lo * xb[c] + hi * xb[c + CHUNKS]
                acc = p if acc is None else acc + p
            q_scr[slot, pl.ds(grp * SUBLANES, SUBLANES), :] = acc

    def lane_sums(slot):
        qh, ql = _split_bf16(q_scr[slot])
        s = (lax.dot_general(ones, qh, lane_dims, preferred_element_type=F32)
             + lax.dot_general(ones, ql, lane_dims, preferred_element_type=F32))
        return s[0:1, :]

    def emit(t, s):
        w_ref[pl.ds(t, 1), :] = _gelu_tanh(s) * g_ref[pl.ds(t, 1), :]

    q_scr[...] = jnp.zeros_like(q_scr)
    _gather_rows(idx_ref, 0, tab_ref, gbuf0)

    def pair(i, c):
        t0 = 2 * i
        tp = jnp.maximum(t0 - 2, 0)
        s0 = lane_sums(0)
        s1 = lane_sums(1)
        _gather_rows(idx_ref, t0 + 1, tab_ref, gbuf1)
        dots(t0, gbuf0, 0)
        _gather_rows(idx_ref, jnp.minimum(t0 + 2, tb - 1), tab_ref, gbuf0)
        dots(t0 + 1, gbuf1, 1)
        emit(tp, s0)
        emit(tp + 1, s1)
        return c

    lax.fori_loop(0, tb // 2, pair, 0)
    emit(tb - 2, lane_sums(0))
    emit(tb - 1, lane_sums(1))


def _peer_u(idx_flat, h2, g, tab, tb):
    N, D = h2.shape
    kernel = functools.partial(_peer_u_kernel, tb=tb)
    return pl.pallas_call(
        kernel,
        grid=(N // tb,),
        in_specs=[
            pl.BlockSpec((tb * ROWS,), lambda i: (i,), memory_space=pltpu.SMEM),
            pl.BlockSpec((tb, D), lambda i: (i, 0)),
            pl.BlockSpec((tb, ROWS), lambda i: (i, 0)),
            pl.BlockSpec(memory_space=pltpu.VMEM),
        ],
        out_specs=pl.BlockSpec((tb, ROWS), lambda i: (i, 0)),
        out_shape=jax.ShapeDtypeStruct((N, ROWS), F32),
        scratch_shapes=[pltpu.VMEM((GROUPS, CHUNKS * SUBLANES, LANES), I32),
                        pltpu.VMEM((GROUPS, CHUNKS * SUBLANES, LANES), I32),
                        pltpu.VMEM((2, ROWS, LANES), F32)],
        compiler_params=_cparams("arbitrary"),
        name="peer_u",
    )(idx_flat, h2, g, tab)


def _peer_v_kernel(idx_ref, w_ref, x1_ref, g2_ref, tab_ref, o_ref, gbuf0, gbuf1, *, tb):
    def combine(t, gbuf):
        wrow = w_ref[pl.ds(t, 1), :]
        wb = jnp.broadcast_to(wrow, (ROWS, ROWS)).T
        acc_lo = [None] * CHUNKS
        acc_hi = [None] * CHUNKS
        for grp in range(GROUPS):
            wg = wb[grp * SUBLANES:(grp + 1) * SUBLANES, :]
            for c in range(CHUNKS):
                lo, hi = _unpack(gbuf[grp, pl.ds(c * SUBLANES, SUBLANES), :])
                pl_, ph_ = lo * wg, hi * wg
                acc_lo[c] = pl_ if acc_lo[c] is None else acc_lo[c] + pl_
                acc_hi[c] = ph_ if acc_hi[c] is None else acc_hi[c] + ph_
        y = jnp.concatenate([jnp.sum(a, axis=0, keepdims=True) for a in acc_lo + acc_hi], axis=1)
        o_ref[pl.ds(t, 1), :] = x1_ref[pl.ds(t, 1), :] + g2_ref[0] * y

    _gather_rows(idx_ref, 0, tab_ref, gbuf0)

    def pair(i, c):
        t0 = 2 * i
        _gather_rows(idx_ref, t0 + 1, tab_ref, gbuf1)
        combine(t0, gbuf0)
        _gather_rows(idx_ref, jnp.minimum(t0 + 2, tb - 1), tab_ref, gbuf0)
        combine(t0 + 1, gbuf1)
        return c

    lax.fori_loop(0, tb // 2, pair, 0)


def _peer_v(idx_flat, w, x1, gate2, tab, tb, seq):
    N, D = x1.shape
    kernel = functools.partial(_peer_v_kernel, tb=tb)
    per_seq = seq // tb
    return pl.pallas_call(
        kernel,
        grid=(N // tb,),
        in_specs=[
            pl.BlockSpec((tb * ROWS,), lambda i: (i,), memory_space=pltpu.SMEM),
            pl.BlockSpec((tb, ROWS), lambda i: (i, 0)),
            pl.BlockSpec((tb, D), lambda i: (i, 0)),
            pl.BlockSpec((1, 1, D), lambda i: (i // per_seq, 0, 0)),
            pl.BlockSpec(memory_space=pltpu.VMEM),
        ],
        out_specs=pl.BlockSpec((tb, D), lambda i: (i, 0)),
        out_shape=jax.ShapeDtypeStruct((N, D), F32),
        scratch_shapes=[pltpu.VMEM((GROUPS, CHUNKS * SUBLANES, LANES), I32),
                        pltpu.VMEM((GROUPS, CHUNKS * SUBLANES, LANES), I32)],
        compiler_params=_cparams("arbitrary"),
        name="peer_v",
    )(idx_flat, w, x1, gate2, tab)


def _pack_table(tab):
    E, D = tab.shape
    half = D // 2
    bits = lax.bitcast_convert_type(tab.astype(BF16), jnp.uint16).astype(jnp.uint32)
    word = bits[:, :half] | (bits[:, half:] << 16)
    return lax.bitcast_convert_type(word, I32).reshape(E * (half // LANES), LANES)


def _block_diag(w):
    nb, n, _ = w.shape
    eye = jnp.eye(nb, dtype=w.dtype)
    return (eye[:, None, :, None] * w[:, :, None, :]).reshape(nb * n, nb * n)


def _key_matrix(k1, k2):
    z = jnp.zeros_like(k1)
    return jnp.concatenate([jnp.concatenate([k1, z], -1), jnp.concatenate([z, k2], -1)], axis=1)


def kernel(x, c, w_mod, b_mod, g_norm1, w_in, g_q, g_k, conv_w, conv_b, w_rg, b_rg, w_ig, b_ig,
           lru_lambda, g_out_attn, g_out_lru, w_out, g_norm2, w_pq, sub_keys1, sub_keys2,
           expert_u, expert_v):
    B, S, D = x.shape
    N = B * S
    depth = w_mod.shape[0]
    assert ATTN_W == 4 * LANES and D == 2 * ATTN_W
    ts = min(512, S)
    tq, tk = 128, min(512, S)
    tc = min(512, S)
    tt = 128
    tb = 128
    h_res = x.astype(F32)
    for l in range(depth):
        mod = _mod(c.astype(F32), w_mod[l].astype(F32), b_mod[l].astype(F32))
        shift1, scale1, gate1, shift2, scale2, gate2 = [
            m.reshape(B, 1, D) for m in jnp.split(mod, 6, axis=-1)]

        scale = 1.0 / math.sqrt(HEAD_DIM)
        gq_t = jnp.tile(g_q[l].astype(F32) * scale, ATTN_HEADS).reshape(1, ATTN_W)
        gk_t = jnp.tile(g_k[l].astype(F32), ATTN_HEADS).reshape(1, ATTN_W)
        qkv, lx, lg = _in_proj(h_res, g_norm1[l].astype(F32).reshape(1, D), scale1, shift1,
                               w_in[l].astype(BF16), gq_t, gk_t, ts)

        attn = _attention(qkv, tq, tk)
        W = lx.shape[-1]
        rec = _lru(lx, lg, conv_w[l].astype(F32), conv_b[l].astype(F32).reshape(1, W),
                   _block_diag(w_rg[l]).astype(BF16), b_rg[l].astype(F32).reshape(1, W),
                   _block_diag(w_ig[l]).astype(BF16), b_ig[l].astype(F32).reshape(1, W),
                   lru_lambda[l].astype(F32).reshape(1, W), tc)

        x1, h2, qp = _out_proj(attn, rec, h_res,
                               g_out_attn[l].astype(F32).reshape(1, W),
                               g_out_lru[l].astype(F32).reshape(1, W),
                               w_out[l].astype(BF16), gate1,
                               g_norm2[l].astype(F32).reshape(1, D), scale2, shift2,
                               w_pq[l].astype(BF16), ts)

        kc = _key_matrix(sub_keys1[l], sub_keys2[l]).astype(BF16)
        idx, g = _topk(qp.reshape(N, D), kc, tt)
        idx_flat = idx.reshape(N * ROWS)
        wts = _peer_u(idx_flat, h2.reshape(N, D), g, _pack_table(expert_u[l]), tb)
        out = _peer_v(idx_flat, wts, x1.reshape(N, D), gate2, _pack_table(expert_v[l]), tb, S)
        h_res = out.reshape(B, S, D)
    return h_res.astype(x.dtype)
```

```python
import functools
import math

import jax
import jax.numpy as jnp
from jax import lax
from jax.experimental import pallas as pl
from jax.experimental.pallas import tpu as pltpu

F32 = jnp.float32
BF16 = jnp.bfloat16
I32 = jnp.int32

EPS = 1e-6
ATTN_HEADS = 8
HEAD_DIM = 64
ATTN_W = ATTN_HEADS * HEAD_DIM
LRU_C = 8.0
CONV_W = 4
PEER_HEADS = 8
PEER_DK = 128
N_KEYS = 128
TOPK = 16

LANES = 128
SUBLANES = 8
VMEM_LIMIT = 56 * 1024 * 1024

ATTN_SKIP_LOG = -88.0
ATTN_SUFFIX_BLOCK = 256
NEG_BIG = -1e30


def _cparams(*sem):
    return pltpu.CompilerParams(dimension_semantics=sem, vmem_limit_bytes=VMEM_LIMIT)


def _gelu_tanh(x):
    return 0.5 * x * (1.0 + jnp.tanh(math.sqrt(2.0 / math.pi) * (x + 0.044715 * x * x * x)))


def _softplus(x):
    return jnp.maximum(x, 0.0) + jnp.log1p(jnp.exp(-jnp.abs(x)))


def _sigmoid(x):
    return 1.0 / (1.0 + jnp.exp(-x))


def _split_bf16(v):
    hi = v.astype(BF16)
    lo = (v - hi.astype(F32)).astype(BF16)
    return hi, lo


def _mod_kernel(c_ref, w_ref, b_ref, o_ref):
    c = c_ref[...]
    ca = c * _sigmoid(c)
    o_ref[...] = jnp.dot(ca, w_ref[...], preferred_element_type=F32) + b_ref[...]


def _mod(c, w_mod, b_mod):
    B, D = c.shape
    n_out = w_mod.shape[1]
    bn = 1024
    return pl.pallas_call(
        _mod_kernel,
        grid=(n_out // bn,),
        in_specs=[
            pl.BlockSpec((B, D), lambda j: (0, 0)),
            pl.BlockSpec((D, bn), lambda j: (0, j)),
            pl.BlockSpec((1, bn), lambda j: (0, j)),
        ],
        out_specs=pl.BlockSpec((B, bn), lambda j: (0, j)),
        out_shape=jax.ShapeDtypeStruct((B, n_out), F32),
        compiler_params=_cparams("arbitrary"),
        name="mod",
    )(c, w_mod, b_mod.reshape(1, n_out))


def _pair_rmsnorm(t, g):
    lane = lax.broadcasted_iota(I32, t.shape, 1)
    first = lane < HEAD_DIM
    sq = t * t
    m_lo = jnp.sum(jnp.where(first, sq, 0.0), axis=-1, keepdims=True)
    m_hi = jnp.sum(jnp.where(first, 0.0, sq), axis=-1, keepdims=True)
    ms = jnp.where(first, m_lo, m_hi) * (1.0 / HEAD_DIM)
    return t * lax.rsqrt(ms + EPS) * g


def _in_proj_kernel(x_ref, g1_ref, sc_ref, sh_ref, w_ref, gq_ref, gk_ref,
                    qkv_ref, lx_ref, lg_ref):
    xf = x_ref[0]
    ms = jnp.mean(xf * xf, axis=-1, keepdims=True)
    h = xf * lax.rsqrt(ms + EPS) * g1_ref[...]
    h = (h * (1.0 + sc_ref[0]) + sh_ref[0]).astype(BF16)
    for cidx in range(5):
        p = jnp.dot(h, w_ref[:, cidx * ATTN_W:(cidx + 1) * ATTN_W], preferred_element_type=F32)
        if cidx < 2:
            g_ref = gq_ref if cidx == 0 else gk_ref
            for j in range(ATTN_W // LANES):
                sl = slice(j * LANES, (j + 1) * LANES)
                t = _pair_rmsnorm(p[:, sl], g_ref[:, sl])
                qkv_ref[0, :, cidx * ATTN_W + j * LANES:cidx * ATTN_W + (j + 1) * LANES] = t.astype(BF16)
        elif cidx == 2:
            qkv_ref[0, :, 2 * ATTN_W:3 * ATTN_W] = p.astype(BF16)
        elif cidx == 3:
            lx_ref[0] = p
        else:
            lg_ref[0] = p


def _in_proj(x, g1, scale1, shift1, w_in_bf, gq_t, gk_t, ts):
    B, S, D = x.shape
    n_cols = w_in_bf.shape[1]
    lw = n_cols - 3 * ATTN_W
    assert lw == 2 * ATTN_W
    vec = lambda: pl.BlockSpec((1, 1, D), lambda b, i: (b, 0, 0))
    return pl.pallas_call(
        _in_proj_kernel,
        grid=(B, S // ts),
        in_specs=[
            pl.BlockSpec((1, ts, D), lambda b, i: (b, i, 0)),
            pl.BlockSpec((1, D), lambda b, i: (0, 0)),
            vec(), vec(),
            pl.BlockSpec((D, n_cols), lambda b, i: (0, 0)),
            pl.BlockSpec((1, ATTN_W), lambda b, i: (0, 0)),
            pl.BlockSpec((1, ATTN_W), lambda b, i: (0, 0)),
        ],
        out_specs=[
            pl.BlockSpec((1, ts, 3 * ATTN_W), lambda b, i: (b, i, 0)),
            pl.BlockSpec((1, ts, ATTN_W), lambda b, i: (b, i, 0)),
            pl.BlockSpec((1, ts, ATTN_W), lambda b, i: (b, i, 0)),
        ],
        out_shape=[
            jax.ShapeDtypeStruct((B, S, 3 * ATTN_W), BF16),
            jax.ShapeDtypeStruct((B, S, ATTN_W), F32),
            jax.ShapeDtypeStruct((B, S, ATTN_W), F32),
        ],
        compiler_params=_cparams("arbitrary", "arbitrary"),
        name="in_proj",
    )(x, g1, scale1, shift1, w_in_bf, gq_t, gk_t)


def _attn_kernel(q_ref, k_ref, v_ref, o_ref, *, tq, tk):
    i = pl.program_id(2)
    t0 = i * tq
    lane = lax.broadcasted_iota(I32, (tq, LANES), 1)
    first = lane < HEAD_DIM
    q = q_ref[0]
    zero = jnp.zeros_like(q)
    qs = jnp.concatenate([jnp.where(first, q, zero), jnp.where(first, zero, q)], axis=0)
    sb = ATTN_SUFFIX_BLOCK
    rj = lax.broadcasted_iota(I32, (sb, sb), 0)
    cs = lax.broadcasted_iota(I32, (sb, sb), 1)
    suffix = jnp.where(rj > cs, 1.0, 0.0).astype(BF16)
    qpos = t0 + (lax.broadcasted_iota(I32, (2 * tq, tk), 0) & (tq - 1))
    kiota = lax.broadcasted_iota(I32, (2 * tq, tk), 1)

    def cond(st):
        end, go, _, _ = st
        return jnp.logical_and(end > 0, go > 0)

    def step(st):
        end, _, carry, acc = st
        k0 = pl.multiple_of(jnp.maximum(end - tk, 0), tq)
        kj = k_ref[0, pl.ds(k0, tk), :]
        vj = v_ref[0, pl.ds(k0, tk), :]
        kpos = k0 + kiota
        valid = jnp.logical_and(kpos < qpos, kpos < end)
        z = lax.dot_general(qs, kj, (((1,), (1,)), ((), ())), preferred_element_type=F32)
        z = jnp.where(valid, z, NEG_BIG)
        sp = _softplus(z)
        lf = -sp
        ls = z - sp
        after = carry
        parts = []
        for b in reversed(range(tk // sb)):
            lfb = lf[:, b * sb:(b + 1) * sb]
            lf_hi, lf_lo = _split_bf16(lfb)
            inside = (jnp.dot(lf_hi, suffix, preferred_element_type=F32)
                      + jnp.dot(lf_lo, suffix, preferred_element_type=F32))
            parts.append(inside + after)
            after = after + jnp.sum(lfb, axis=-1, keepdims=True)
        later = jnp.concatenate(parts[::-1], axis=1)
        w = jnp.exp(ls + later)
        acc = acc + jnp.dot(w.astype(BF16), vj, preferred_element_type=F32)
        carry = after
        go = (jnp.max(carry) >= ATTN_SKIP_LOG).astype(I32)
        return end - tk, go, carry, acc

    st0 = (t0 + tq, jnp.int32(1), jnp.zeros((2 * tq, 1), F32), jnp.zeros((2 * tq, LANES), F32))
    acc = lax.while_loop(cond, step, st0)[3]
    o_ref[0] = jnp.where(first, acc[0:tq], acc[tq:2 * tq])


def _attention(qkv, tq, tk):
    B, S, _ = qkv.shape
    n_pairs = ATTN_W // LANES
    kernel = functools.partial(_attn_kernel, tq=tq, tk=tk)
    return pl.pallas_call(
        kernel,
        grid=(B, n_pairs, S // tq),
        in_specs=[
            pl.BlockSpec((1, tq, LANES), lambda b, p, i: (b, i, p)),
            pl.BlockSpec((1, S, LANES), lambda b, p, i: (b, 0, n_pairs + p)),
            pl.BlockSpec((1, S, LANES), lambda b, p, i: (b, 0, 2 * n_pairs + p)),
        ],
        out_specs=pl.BlockSpec((1, tq, LANES), lambda b, p, i: (b, i, p)),
        out_shape=jax.ShapeDtypeStruct((B, S, ATTN_W), F32),
        compiler_params=_cparams("arbitrary", "arbitrary", "arbitrary"),
        name="attn",
    )(qkv, qkv, qkv)


def _shift_rows(v, d, fill):
    rows = lax.broadcasted_iota(I32, v.shape, 0)
    return jnp.where(rows >= d, pltpu.roll(v, d, axis=0), fill)


def _lru_kernel(lx_ref, lg_ref, cw_ref, cb_ref, wr_ref, br_ref, wi_ref, bi_ref, lam_ref,
                o_ref, xs_ref, h_ref, *, tc):
    halo = SUBLANES

    @pl.when(pl.program_id(1) == 0)
    def _():
        xs_ref[0:halo, :] = jnp.zeros((halo, ATTN_W), F32)
        h_ref[...] = jnp.zeros_like(h_ref)

    x = lx_ref[0]
    xs_ref[halo:halo + tc, :] = x
    xb = cb_ref[...] + x * cw_ref[CONV_W - 1:CONV_W, :]
    for tap in range(CONV_W - 1):
        d = CONV_W - 1 - tap
        xb = xb + xs_ref[halo - d:halo - d + tc, :] * cw_ref[tap:tap + 1, :]
    xs_ref[0:halo, :] = x[tc - halo:tc, :]

    xbb = xb.astype(BF16)
    r = _sigmoid(jnp.dot(xbb, wr_ref[...], preferred_element_type=F32) + br_ref[...])
    ig = _sigmoid(jnp.dot(xbb, wi_ref[...], preferred_element_type=F32) + bi_ref[...])
    log_a = -LRU_C * r * _softplus(-lam_ref[...])
    a = jnp.exp(log_a)
    u = jnp.sqrt(jnp.tanh(-log_a) * (1.0 + a * a)) * (ig * xb)

    d = 1
    while d < tc:
        a_prev = _shift_rows(a, d, 1.0)
        u_prev = _shift_rows(u, d, 0.0)
        u = a * u_prev + u
        a = a * a_prev
        d *= 2
    h = a * h_ref[...] + u
    h_ref[...] = h[tc - 1:tc, :]
    o_ref[0] = h * _gelu_tanh(lg_ref[0])


def _lru(lx, lg, conv_w, conv_b, wr_bd, b_rg, wi_bd, b_ig, lam, tc):
    B, S, W = lx.shape
    row = lambda: pl.BlockSpec((1, W), lambda b, i: (0, 0))
    kernel = functools.partial(_lru_kernel, tc=tc)
    return pl.pallas_call(
        kernel,
        grid=(B, S // tc),
        in_specs=[
            pl.BlockSpec((1, tc, W), lambda b, i: (b, i, 0)),
            pl.BlockSpec((1, tc, W), lambda b, i: (b, i, 0)),
            pl.BlockSpec((CONV_W, W), lambda b, i: (0, 0)),
            row(),
            pl.BlockSpec((W, W), lambda b, i: (0, 0)),
            row(),
            pl.BlockSpec((W, W), lambda b, i: (0, 0)),
            row(), row(),
        ],
        out_specs=pl.BlockSpec((1, tc, W), lambda b, i: (b, i, 0)),
        out_shape=jax.ShapeDtypeStruct((B, S, W), F32),
        scratch_shapes=[pltpu.VMEM((tc + SUBLANES, W), F32), pltpu.VMEM((1, W), F32)],
        compiler_params=_cparams("arbitrary", "arbitrary"),
        name="lru",
    )(lx, lg, conv_w, conv_b, wr_bd, b_rg, wi_bd, b_ig, lam)


def _rms(v, g):
    ms = jnp.mean(v * v, axis=-1, keepdims=True)
    return v * lax.rsqrt(ms + EPS) * g


def _out_proj_kernel(at_ref, rc_ref, x_ref, ga_ref, gr_ref, wo_ref, g1_ref, g2_ref,
                     sc_ref, sh_ref, wq_ref, x1_ref, h2_ref, qp_ref):
    ma = _rms(at_ref[0], ga_ref[...]).astype(BF16)
    mr = _rms(rc_ref[0], gr_ref[...]).astype(BF16)
    y = (jnp.dot(ma, wo_ref[0:ATTN_W, :], preferred_element_type=F32)
         + jnp.dot(mr, wo_ref[ATTN_W:2 * ATTN_W, :], preferred_element_type=F32))
    x1 = x_ref[0] + g1_ref[0] * y
    x1_ref[0] = x1
    h2 = _rms(x1, g2_ref[...]) * (1.0 + sc_ref[0]) + sh_ref[0]
    h2_ref[0] = h2
    qp_ref[0] = jnp.dot(h2.astype(BF16), wq_ref[...], preferred_element_type=F32).astype(BF16)


def _out_proj(attn, rec, x, g_oa, g_ol, w_out_bf, gate1, g2, scale2, shift2, w_pq_bf, ts):
    B, S, D = x.shape
    W = attn.shape[-1]
    tok = lambda w: pl.BlockSpec((1, ts, w), lambda b, i: (b, i, 0))
    vec = lambda: pl.BlockSpec((1, 1, D), lambda b, i: (b, 0, 0))
    full = lambda r, c: pl.BlockSpec((r, c), lambda b, i: (0, 0))
    return pl.pallas_call(
        _out_proj_kernel,
        grid=(B, S // ts),
        in_specs=[tok(W), tok(W), tok(D), full(1, W), full(1, W), full(D, D), vec(),
                  full(1, D), vec(), vec(), full(D, D)],
        out_specs=[tok(D), tok(D), tok(D)],
        out_shape=[jax.ShapeDtypeStruct((B, S, D), F32),
                   jax.ShapeDtypeStruct((B, S, D), F32),
                   jax.ShapeDtypeStruct((B, S, D), BF16)],
        compiler_params=_cparams("arbitrary", "arbitrary"),
        name="out_proj",
    )(attn, rec, x, g_oa, g_ol, w_out_bf, gate1, g2, scale2, shift2, w_pq_bf)


def _tree(xs, op):
    xs = list(xs)
    while len(xs) > 1:
        nxt = [op(xs[k], xs[k + 1]) for k in range(0, len(xs) - 1, 2)]
        if len(xs) % 2:
            nxt.append(xs[-1])
        xs = nxt
    return xs[0]


def _all_sublanes(x, op):
    for s in (4, 2, 1):
        x = op(x, pltpu.roll(x, s, axis=0))
    return x


def _top_of_pairs(problems, rounds):
    neg_inf = jnp.float32(-jnp.inf)
    shape = problems[0][0][0][0].shape
    row8 = lax.broadcasted_iota(I32, shape, 0)
    big = jnp.int32(TOPK * TOPK)
    levels = len(problems[0][0])
    mv = [[t[0] for t in main] for main, _ in problems]
    me = [[t[1] for t in main] for main, _ in problems]
    sv = [[t[0] for t in side] for _, side in problems]
    se = [[t[1] for t in side] for _, side in problems]
    sp = [[t[2] for t in side] for _, side in problems]
    mpos = [row8 * TOPK for _ in problems]
    out_v = [[jnp.zeros(shape, F32) for _ in range(rounds // SUBLANES)] for _ in problems]
    out_e = [[jnp.zeros(shape, I32) for _ in range(rounds // SUBLANES)] for _ in problems]
    for r in range(rounds):
        here = row8 == (r % SUBLANES)
        depth = min(rounds - r, levels)
        for p in range(len(problems)):
            heads = [mv[p][0]] + sv[p]
            poss = [mpos[p]] + sp[p]
            eids = [me[p][0]] + se[p]
            m = _all_sublanes(_tree(heads, jnp.maximum), jnp.maximum)
            first = [jnp.where(h == m, q, big) for h, q in zip(heads, poss)]
            sel = _all_sublanes(_tree(first, jnp.minimum), jnp.minimum)
            wins = [(q == sel) & (h == m) for q, h in zip(poss, heads)]
            taken = _all_sublanes(
                _tree([jnp.where(w, e, -1) for w, e in zip(wins, eids)], jnp.maximum), jnp.maximum)
            out_v[p][r // SUBLANES] = jnp.where(here, m, out_v[p][r // SUBLANES])
            out_e[p][r // SUBLANES] = jnp.where(here, taken, out_e[p][r // SUBLANES])
            for k in range(len(sv[p])):
                sv[p][k] = jnp.where(wins[k + 1], neg_inf, sv[p][k])
            win = wins[0]
            mpos[p] = jnp.where(win, mpos[p] + 1, mpos[p])
            for lvl in range(depth - 1):
                mv[p][lvl] = jnp.where(win, mv[p][lvl + 1], mv[p][lvl])
                me[p][lvl] = jnp.where(win, me[p][lvl + 1], me[p][lvl])
            if depth == levels:
                mv[p][levels - 1] = jnp.where(win, neg_inf, mv[p][levels - 1])
    return [(out_v[p], out_e[p]) for p in range(len(problems))]


def _sorting_network(n):
    pairs = []
    p = 1
    while p < n:
        k = p
        while k >= 1:
            for j in range(k % p, n - k, 2 * k):
                for i in range(min(k, n - j - k)):
                    if (i + j) // (2 * p) == (i + j + k) // (2 * p):
                        pairs.append((i + j, i + j + k))
            k //= 2
        p *= 2
    return pairs


def _top_of_keys(problems, ids, rounds):
    n = len(ids)
    shape = ids[0].shape
    row8 = lax.broadcasted_iota(I32, shape, 0)
    big = jnp.int32(n * SUBLANES)
    vals = [list(v) for v in problems]
    keys = [list(ids) for _ in problems]
    for i, j in _sorting_network(n):
        for v, k in zip(vals, keys):
            swap = (v[j] > v[i]) | ((v[j] == v[i]) & (k[j] < k[i]))
            v[i], v[j] = jnp.where(swap, v[j], v[i]), jnp.where(swap, v[i], v[j])
            k[i], k[j] = jnp.where(swap, k[j], k[i]), jnp.where(swap, k[i], k[j])
    out_v = [[jnp.zeros(shape, F32) for _ in range(rounds // SUBLANES)] for _ in problems]
    out_k = [[jnp.zeros(shape, I32) for _ in range(rounds // SUBLANES)] for _ in problems]
    for r in range(rounds):
        here = row8 == (r % SUBLANES)
        depth = min(rounds - r, n)
        for p, (v, k) in enumerate(zip(vals, keys)):
            m = _all_sublanes(v[0], jnp.maximum)
            sel = _all_sublanes(jnp.where(v[0] == m, k[0], big), jnp.minimum)
            win = k[0] == sel
            out_v[p][r // SUBLANES] = jnp.where(here, m, out_v[p][r // SUBLANES])
            out_k[p][r // SUBLANES] = jnp.where(here, sel, out_k[p][r // SUBLANES])
            for lvl in range(depth - 1):
                v[lvl] = jnp.where(win, v[lvl + 1], v[lvl])
                k[lvl] = jnp.where(win, k[lvl + 1], k[lvl])
    return [(out_v[p], out_k[p]) for p in range(len(problems))]


def _row(x, r):
    return jnp.broadcast_to(x[r:r + 1, :], x.shape)


def _candidates(v1, v2, i1, i2):
    neg_inf = jnp.float32(-jnp.inf)
    row8 = lax.broadcasted_iota(I32, v1[0].shape, 0)
    main = []
    for b in range(SUBLANES):
        s = v1[0] + _row(v2[0], b)
        n_valid = TOPK // (b + 1)
        if n_valid < SUBLANES:
            s = jnp.where(row8 < n_valid, s, neg_inf)
        main.append((s, i1[0] * N_KEYS + _row(i2[0], b)))
    side = [
        (v1[1] + _row(v2[0], 0), i1[1] * N_KEYS + _row(i2[0], 0), (row8 + SUBLANES) * TOPK),
        (_row(v1[0], 0) + v2[1], _row(i1[0], 0) * N_KEYS + i2[1], row8 + SUBLANES),
    ]
    return main, side


def _topk_kernel(qp_ref, kc_ref, idx_ref, g_ref, e_scr, g_scr, *, tt):
    row8 = lax.broadcasted_iota(I32, (SUBLANES, tt), 0)
    kid = [row8 + k * SUBLANES for k in range(N_KEYS // SUBLANES)]
    n_tiles = N_KEYS // SUBLANES

    def head_pair(hp, c):
        cands = []
        for k in range(2):
            h = 2 * hp + k
            c0 = pl.multiple_of(h * PEER_DK, PEER_DK)
            qh = qp_ref[:, pl.ds(c0, PEER_DK)]
            st = lax.dot_general(kc_ref[h], qh, (((1,), (1,)), ((), ())),
                                 preferred_element_type=F32)
            tiles = [st[j * SUBLANES:(j + 1) * SUBLANES, :] for j in range(2 * n_tiles)]
            (v1, i1), (v2, i2) = _top_of_keys([tiles[:n_tiles], tiles[n_tiles:]], kid, TOPK)
            cands.append(_candidates(v1, v2, i1, i2))
        tops = _top_of_pairs(cands, TOPK)
        for k, (top_s, top_e) in enumerate(tops):
            best = _row(top_s[0], 0)
            ex = [jnp.exp(t - best) for t in top_s]
            denom = _all_sublanes(ex[0] + ex[1], jnp.add)
            r0 = (2 * hp + k) * TOPK
            for j in range(TOPK // SUBLANES):
                rj = pl.multiple_of(r0 + j * SUBLANES, SUBLANES)
                e_scr[pl.ds(rj, SUBLANES), :] = top_e[j] * 4
                g_scr[pl.ds(rj, SUBLANES), :] = ex[j] / denom
        return c

    lax.fori_loop(0, PEER_HEADS // 2, head_pair, 0)
    idx_ref[...] = e_scr[...].T
    g_ref[...] = g_scr[...].T


def _topk(qp2, kc, tt):
    N, D = qp2.shape
    kernel = functools.partial(_topk_kernel, tt=tt)
    slots = PEER_HEADS * TOPK
    return pl.pallas_call(
        kernel,
        grid=(N // tt,),
        in_specs=[
            pl.BlockSpec((tt, D), lambda i: (i, 0)),
            pl.BlockSpec(kc.shape, lambda i: (0, 0, 0)),
        ],
        out_specs=[pl.BlockSpec((tt, slots), lambda i: (i, 0)),
                   pl.BlockSpec((tt, slots), lambda i: (i, 0))],
        out_shape=[jax.ShapeDtypeStruct((N, slots), I32),
                   jax.ShapeDtypeStruct((N, slots), F32)],
        scratch_shapes=[pltpu.VMEM((slots, tt), I32), pltpu.VMEM((slots, tt), F32)],
        compiler_params=_cparams("arbitrary"),
        name="topk",
    )(qp2, kc)


ROWS = PEER_HEADS * TOPK
GROUPS = ROWS // SUBLANES
CHUNKS = 4
IDX_PARTS = 1
PART_ROWS = ROWS // IDX_PARTS


def _gather_rows(idx_refs, t, tab_ref, gbuf):
    base = t * PART_ROWS
    for k in range(PART_ROWS):
        off = base + k
        for p, idx_ref in enumerate(idx_refs):
            grp, j = divmod(p * PART_ROWS + k, SUBLANES)
            e = idx_ref[off]
            gbuf[grp, pl.ds(j, CHUNKS, stride=SUBLANES), :] = (
                tab_ref[pl.ds(pl.multiple_of(e, CHUNKS), CHUNKS), :])


def _idx_specs(tb):
    return [pl.BlockSpec((tb * PART_ROWS,), lambda i: (i,), memory_space=pltpu.SMEM)
            for _ in range(IDX_PARTS)]


def _unpack(w):
    lo = pltpu.bitcast(w << 16, F32)
    hi = pltpu.bitcast(w & jnp.int32(-65536), F32)
    return lo, hi


def _peer_u_kernel(*refs, tb):
    idx_ref = refs[:IDX_PARTS]
    x_ref, g_ref, tab_ref, w_ref, gbuf0, gbuf1, q_scr = refs[IDX_PARTS:]
    ones_sq = jnp.ones((LANES, LANES), BF16)
    diag = (lax.broadcasted_iota(I32, (ROWS, LANES), 0)
            == lax.broadcasted_iota(I32, (ROWS, LANES), 1))

    def dots(t, gbuf, slot):
        xrow = x_ref[pl.ds(t, 1), :]
        xb = [jnp.broadcast_to(xrow[:, c * LANES:(c + 1) * LANES], (SUBLANES, LANES))
              for c in range(2 * CHUNKS)]
        for grp in range(GROUPS):
            acc = None
            for c in range(CHUNKS):
                lo, hi = _unpack(gbuf[grp, pl.ds(c * SUBLANES, SUBLANES), :])
                p = lo * xb[c] + hi * xb[c + CHUNKS]
                acc = p if acc is None else acc + p
            q_scr[slot, pl.ds(grp * SUBLANES, SUBLANES), :] = acc

    def lane_sums(slot):
        qh, ql = _split_bf16(q_scr[slot])
        full = (jnp.dot(qh, ones_sq, preferred_element_type=F32)
                + jnp.dot(ql, ones_sq, preferred_element_type=F32))
        return jnp.sum(jnp.where(diag, full, 0.0), axis=0, keepdims=True)

    def emit(t, s):
        w_ref[pl.ds(t, 1), :] = _gelu_tanh(s) * g_ref[pl.ds(t, 1), :]

    q_scr[...] = jnp.zeros_like(q_scr)
    _gather_rows(idx_ref, 0, tab_ref, gbuf0)

    def pair(i, c):
        t0 = 2 * i
        tp = jnp.maximum(t0 - 2, 0)
        s0 = lane_sums(0)
        s1 = lane_sums(1)
        _gather_rows(idx_ref, t0 + 1, tab_ref, gbuf1)
        dots(t0, gbuf0, 0)
        _gather_rows(idx_ref, jnp.minimum(t0 + 2, tb - 1), tab_ref, gbuf0)
        dots(t0 + 1, gbuf1, 1)
        emit(tp, s0)
        emit(tp + 1, s1)
        return c

    lax.fori_loop(0, tb // 2, pair, 0)
    emit(tb - 2, lane_sums(0))
    emit(tb - 1, lane_sums(1))


def _peer_u(idx_heads, h2, g, tab, tb):
    N, D = h2.shape
    kernel = functools.partial(_peer_u_kernel, tb=tb)
    return pl.pallas_call(
        kernel,
        grid=(N // tb,),
        in_specs=_idx_specs(tb) + [
            pl.BlockSpec((tb, D), lambda i: (i, 0)),
            pl.BlockSpec((tb, ROWS), lambda i: (i, 0)),
            pl.BlockSpec(memory_space=pltpu.VMEM),
        ],
        out_specs=pl.BlockSpec((tb, ROWS), lambda i: (i, 0)),
        out_shape=jax.ShapeDtypeStruct((N, ROWS), F32),
        scratch_shapes=[pltpu.VMEM((GROUPS, CHUNKS * SUBLANES, LANES), I32),
                        pltpu.VMEM((GROUPS, CHUNKS * SUBLANES, LANES), I32),
                        pltpu.VMEM((2, ROWS, LANES), F32)],
        compiler_params=_cparams("arbitrary"),
        name="peer_u",
    )(*idx_heads, h2, g, tab)


def _peer_v_kernel(*refs, tb):
    idx_ref = refs[:IDX_PARTS]
    w_ref, x1_ref, g2_ref, tab_ref, o_ref, gbuf0, gbuf1 = refs[IDX_PARTS:]
    def combine(t, gbuf):
        wrow = w_ref[pl.ds(t, 1), :]
        wb = jnp.broadcast_to(wrow, (ROWS, ROWS)).T
        acc_lo = [None] * CHUNKS
        acc_hi = [None] * CHUNKS
        for grp in range(GROUPS):
            wg = wb[grp * SUBLANES:(grp + 1) * SUBLANES, :]
            for c in range(CHUNKS):
                lo, hi = _unpack(gbuf[grp, pl.ds(c * SUBLANES, SUBLANES), :])
                pl_, ph_ = lo * wg, hi * wg
                acc_lo[c] = pl_ if acc_lo[c] is None else acc_lo[c] + pl_
                acc_hi[c] = ph_ if acc_hi[c] is None else acc_hi[c] + ph_
        y = jnp.concatenate([jnp.sum(a, axis=0, keepdims=True) for a in acc_lo + acc_hi], axis=1)
        o_ref[pl.ds(t, 1), :] = x1_ref[pl.ds(t, 1), :] + g2_ref[0] * y

    _gather_rows(idx_ref, 0, tab_ref, gbuf0)

    def pair(i, c):
        t0 = 2 * i
        _gather_rows(idx_ref, t0 + 1, tab_ref, gbuf1)
        combine(t0, gbuf0)
        _gather_rows(idx_ref, jnp.minimum(t0 + 2, tb - 1), tab_ref, gbuf0)
        combine(t0 + 1, gbuf1)
        return c

    lax.fori_loop(0, tb // 2, pair, 0)


def _peer_v(idx_heads, w, x1, gate2, tab, tb, seq):
    N, D = x1.shape
    kernel = functools.partial(_peer_v_kernel, tb=tb)
    per_seq = seq // tb
    return pl.pallas_call(
        kernel,
        grid=(N // tb,),
        in_specs=_idx_specs(tb) + [
            pl.BlockSpec((tb, ROWS), lambda i: (i, 0)),
            pl.BlockSpec((tb, D), lambda i: (i, 0)),
            pl.BlockSpec((1, 1, D), lambda i: (i // per_seq, 0, 0)),
            pl.BlockSpec(memory_space=pltpu.VMEM),
        ],
        out_specs=pl.BlockSpec((tb, D), lambda i: (i, 0)),
        out_shape=jax.ShapeDtypeStruct((N, D), F32),
        scratch_shapes=[pltpu.VMEM((GROUPS, CHUNKS * SUBLANES, LANES), I32),
                        pltpu.VMEM((GROUPS, CHUNKS * SUBLANES, LANES), I32)],
        compiler_params=_cparams("arbitrary"),
        name="peer_v",
    )(*idx_heads, w, x1, gate2, tab)


def _pack_table(tab):
    E, D = tab.shape
    half = D // 2
    bits = lax.bitcast_convert_type(tab.astype(BF16), jnp.uint16).astype(jnp.uint32)
    word = bits[:, :half] | (bits[:, half:] << 16)
    return lax.bitcast_convert_type(word, I32).reshape(E * (half // LANES), LANES)


def _block_diag(w):
    nb, n, _ = w.shape
    eye = jnp.eye(nb, dtype=w.dtype)
    return (eye[:, None, :, None] * w[:, :, None, :]).reshape(nb * n, nb * n)


def _key_matrix(k1, k2):
    z = jnp.zeros_like(k1)
    return jnp.concatenate([jnp.concatenate([k1, z], -1), jnp.concatenate([z, k2], -1)], axis=1)


def kernel(x, c, w_mod, b_mod, g_norm1, w_in, g_q, g_k, conv_w, conv_b, w_rg, b_rg, w_ig, b_ig,
           lru_lambda, g_out_attn, g_out_lru, w_out, g_norm2, w_pq, sub_keys1, sub_keys2,
           expert_u, expert_v):
    B, S, D = x.shape
    N = B * S
    depth = w_mod.shape[0]
    assert ATTN_W == 4 * LANES and D == 2 * ATTN_W
    ts = min(512, S)
    tq, tk = 128, min(512, S)
    tc = min(512, S)
    tt = 128
    tb = 128
    h_res = x.astype(F32)
    for l in range(depth):
        mod = _mod(c.astype(F32), w_mod[l].astype(F32), b_mod[l].astype(F32))
        shift1, scale1, gate1, shift2, scale2, gate2 = [
            m.reshape(B, 1, D) for m in jnp.split(mod, 6, axis=-1)]

        scale = 1.0 / math.sqrt(HEAD_DIM)
        gq_t = jnp.tile(g_q[l].astype(F32) * scale, ATTN_HEADS).reshape(1, ATTN_W)
        gk_t = jnp.tile(g_k[l].astype(F32), ATTN_HEADS).reshape(1, ATTN_W)
        qkv, lx, lg = _in_proj(h_res, g_norm1[l].astype(F32).reshape(1, D), scale1, shift1,
                               w_in[l].astype(BF16), gq_t, gk_t, ts)

        attn = _attention(qkv, tq, tk)
        W = lx.shape[-1]
        rec = _lru(lx, lg, conv_w[l].astype(F32), conv_b[l].astype(F32).reshape(1, W),
                   _block_diag(w_rg[l]).astype(BF16), b_rg[l].astype(F32).reshape(1, W),
                   _block_diag(w_ig[l]).astype(BF16), b_ig[l].astype(F32).reshape(1, W),
                   lru_lambda[l].astype(F32).reshape(1, W), tc)

        x1, h2, qp = _out_proj(attn, rec, h_res,
                               g_out_attn[l].astype(F32).reshape(1, W),
                               g_out_lru[l].astype(F32).reshape(1, W),
                               w_out[l].astype(BF16), gate1,
                               g_norm2[l].astype(F32).reshape(1, D), scale2, shift2,
                               w_pq[l].astype(BF16), ts)

        kc = _key_matrix(sub_keys1[l], sub_keys2[l]).astype(BF16)
        idx, g = _topk(qp.reshape(N, D), kc, tt)
        idx_heads = [idx[:, p * PART_ROWS:(p + 1) * PART_ROWS].reshape(N * PART_ROWS)
                     for p in range(IDX_PARTS)]
        wts = _peer_u(idx_heads, h2.reshape(N, D), g, _pack_table(expert_u[l]), tb)
        out = _peer_v(idx_heads, wts, x1.reshape(N, D), gate2, _pack_table(expert_v[l]), tb, S)
        h_res = out.reshape(B, S, D)
    return h_res.astype(x.dtype)
```

```python
import functools
import math

import jax
import jax.numpy as jnp
from jax import lax
from jax.experimental import pallas as pl
from jax.experimental.pallas import tpu as pltpu

F32 = jnp.float32
BF16 = jnp.bfloat16
I32 = jnp.int32

EPS = 1e-6
ATTN_HEADS = 8
HEAD_DIM = 64
ATTN_W = ATTN_HEADS * HEAD_DIM
LRU_C = 8.0
CONV_W = 4
PEER_HEADS = 8
PEER_DK = 128
N_KEYS = 128
TOPK = 16

LANES = 128
SUBLANES = 8
VMEM_LIMIT = 56 * 1024 * 1024

ATTN_SKIP_LOG = -88.0
ATTN_SUFFIX_BLOCK = 256
NEG_BIG = -1e30


def _cparams(*sem):
    return pltpu.CompilerParams(dimension_semantics=sem, vmem_limit_bytes=VMEM_LIMIT)


def _gelu_tanh(x):
    return 0.5 * x * (1.0 + jnp.tanh(math.sqrt(2.0 / math.pi) * (x + 0.044715 * x * x * x)))


def _softplus(x):
    return jnp.maximum(x, 0.0) + jnp.log1p(jnp.exp(-jnp.abs(x)))


def _sigmoid(x):
    return 1.0 / (1.0 + jnp.exp(-x))


def _split_bf16(v):
    hi = v.astype(BF16)
    lo = (v - hi.astype(F32)).astype(BF16)
    return hi, lo


def _mod_kernel(c_ref, w_ref, b_ref, o_ref):
    c = c_ref[...]
    ca = c * _sigmoid(c)
    o_ref[...] = jnp.dot(ca, w_ref[...], preferred_element_type=F32) + b_ref[...]


def _mod(c, w_mod, b_mod):
    B, D = c.shape
    n_out = w_mod.shape[1]
    bn = 1024
    return pl.pallas_call(
        _mod_kernel,
        grid=(n_out // bn,),
        in_specs=[
            pl.BlockSpec((B, D), lambda j: (0, 0)),
            pl.BlockSpec((D, bn), lambda j: (0, j)),
            pl.BlockSpec((1, bn), lambda j: (0, j)),
        ],
        out_specs=pl.BlockSpec((B, bn), lambda j: (0, j)),
        out_shape=jax.ShapeDtypeStruct((B, n_out), F32),
        compiler_params=_cparams("arbitrary"),
        name="mod",
    )(c, w_mod, b_mod.reshape(1, n_out))


def _pair_rmsnorm(t, g):
    lane = lax.broadcasted_iota(I32, t.shape, 1)
    first = lane < HEAD_DIM
    sq = t * t
    m_lo = jnp.sum(jnp.where(first, sq, 0.0), axis=-1, keepdims=True)
    m_hi = jnp.sum(jnp.where(first, 0.0, sq), axis=-1, keepdims=True)
    ms = jnp.where(first, m_lo, m_hi) * (1.0 / HEAD_DIM)
    return t * lax.rsqrt(ms + EPS) * g


def _in_proj_kernel(x_ref, g1_ref, sc_ref, sh_ref, w_ref, gq_ref, gk_ref,
                    qkv_ref, lx_ref, lg_ref):
    xf = x_ref[0]
    ms = jnp.mean(xf * xf, axis=-1, keepdims=True)
    h = xf * lax.rsqrt(ms + EPS) * g1_ref[...]
    h = (h * (1.0 + sc_ref[0]) + sh_ref[0]).astype(BF16)
    for cidx in range(5):
        p = jnp.dot(h, w_ref[:, cidx * ATTN_W:(cidx + 1) * ATTN_W], preferred_element_type=F32)
        if cidx < 2:
            g_ref = gq_ref if cidx == 0 else gk_ref
            for j in range(ATTN_W // LANES):
                sl = slice(j * LANES, (j + 1) * LANES)
                t = _pair_rmsnorm(p[:, sl], g_ref[:, sl])
                qkv_ref[0, :, cidx * ATTN_W + j * LANES:cidx * ATTN_W + (j + 1) * LANES] = t.astype(BF16)
        elif cidx == 2:
            qkv_ref[0, :, 2 * ATTN_W:3 * ATTN_W] = p.astype(BF16)
        elif cidx == 3:
            lx_ref[0] = p
        else:
            lg_ref[0] = p


def _in_proj(x, g1, scale1, shift1, w_in_bf, gq_t, gk_t, ts):
    B, S, D = x.shape
    n_cols = w_in_bf.shape[1]
    lw = n_cols - 3 * ATTN_W
    assert lw == 2 * ATTN_W
    vec = lambda: pl.BlockSpec((1, 1, D), lambda b, i: (b, 0, 0))
    return pl.pallas_call(
        _in_proj_kernel,
        grid=(B, S // ts),
        in_specs=[
            pl.BlockSpec((1, ts, D), lambda b, i: (b, i, 0)),
            pl.BlockSpec((1, D), lambda b, i: (0, 0)),
            vec(), vec(),
            pl.BlockSpec((D, n_cols), lambda b, i: (0, 0)),
            pl.BlockSpec((1, ATTN_W), lambda b, i: (0, 0)),
            pl.BlockSpec((1, ATTN_W), lambda b, i: (0, 0)),
        ],
        out_specs=[
            pl.BlockSpec((1, ts, 3 * ATTN_W), lambda b, i: (b, i, 0)),
            pl.BlockSpec((1, ts, ATTN_W), lambda b, i: (b, i, 0)),
            pl.BlockSpec((1, ts, ATTN_W), lambda b, i: (b, i, 0)),
        ],
        out_shape=[
            jax.ShapeDtypeStruct((B, S, 3 * ATTN_W), BF16),
            jax.ShapeDtypeStruct((B, S, ATTN_W), F32),
            jax.ShapeDtypeStruct((B, S, ATTN_W), F32),
        ],
        compiler_params=_cparams("arbitrary", "arbitrary"),
        name="in_proj",
    )(x, g1, scale1, shift1, w_in_bf, gq_t, gk_t)


def _attn_kernel(q_ref, k_ref, v_ref, o_ref, *, tq, tk):
    i = pl.program_id(2)
    t0 = i * tq
    lane = lax.broadcasted_iota(I32, (tq, LANES), 1)
    first = lane < HEAD_DIM
    q = q_ref[0]
    zero = jnp.zeros_like(q)
    qs = jnp.concatenate([jnp.where(first, q, zero), jnp.where(first, zero, q)], axis=0)
    sb = ATTN_SUFFIX_BLOCK
    rj = lax.broadcasted_iota(I32, (sb, sb), 0)
    cs = lax.broadcasted_iota(I32, (sb, sb), 1)
    suffix = jnp.where(rj > cs, 1.0, 0.0).astype(BF16)
    qpos = t0 + (lax.broadcasted_iota(I32, (2 * tq, tk), 0) & (tq - 1))
    kiota = lax.broadcasted_iota(I32, (2 * tq, tk), 1)

    def cond(st):
        end, go, _, _ = st
        return jnp.logical_and(end > 0, go > 0)

    def step(st):
        end, _, carry, acc = st
        k0 = pl.multiple_of(jnp.maximum(end - tk, 0), tq)
        kj = k_ref[0, pl.ds(k0, tk), :]
        vj = v_ref[0, pl.ds(k0, tk), :]
        kpos = k0 + kiota
        valid = jnp.logical_and(kpos < qpos, kpos < end)
        z = lax.dot_general(qs, kj, (((1,), (1,)), ((), ())), preferred_element_type=F32)
        z = jnp.where(valid, z, NEG_BIG)
        sp = _softplus(z)
        lf = -sp
        ls = z - sp
        after = carry
        parts = []
        for b in reversed(range(tk // sb)):
            lfb = lf[:, b * sb:(b + 1) * sb]
            lf_hi, lf_lo = _split_bf16(lfb)
            inside = (jnp.dot(lf_hi, suffix, preferred_element_type=F32)
                      + jnp.dot(lf_lo, suffix, preferred_element_type=F32))
            parts.append(inside + after)
            after = after + jnp.sum(lfb, axis=-1, keepdims=True)
        later = jnp.concatenate(parts[::-1], axis=1)
        w = jnp.exp(ls + later)
        acc = acc + jnp.dot(w.astype(BF16), vj, preferred_element_type=F32)
        carry = after
        go = (jnp.max(carry) >= ATTN_SKIP_LOG).astype(I32)
        return end - tk, go, carry, acc

    st0 = (t0 + tq, jnp.int32(1), jnp.zeros((2 * tq, 1), F32), jnp.zeros((2 * tq, LANES), F32))
    acc = lax.while_loop(cond, step, st0)[3]
    o_ref[0] = jnp.where(first, acc[0:tq], acc[tq:2 * tq])


def _attention(qkv, tq, tk):
    B, S, _ = qkv.shape
    n_pairs = ATTN_W // LANES
    kernel = functools.partial(_attn_kernel, tq=tq, tk=tk)
    return pl.pallas_call(
        kernel,
        grid=(B, n_pairs, S // tq),
        in_specs=[
            pl.BlockSpec((1, tq, LANES), lambda b, p, i: (b, i, p)),
            pl.BlockSpec((1, S, LANES), lambda b, p, i: (b, 0, n_pairs + p)),
            pl.BlockSpec((1, S, LANES), lambda b, p, i: (b, 0, 2 * n_pairs + p)),
        ],
        out_specs=pl.BlockSpec((1, tq, LANES), lambda b, p, i: (b, i, p)),
        out_shape=jax.ShapeDtypeStruct((B, S, ATTN_W), F32),
        compiler_params=_cparams("arbitrary", "arbitrary", "arbitrary"),
        name="attn",
    )(qkv, qkv, qkv)


def _shift_rows(v, d, fill):
    rows = lax.broadcasted_iota(I32, v.shape, 0)
    return jnp.where(rows >= d, pltpu.roll(v, d, axis=0), fill)


def _lru_kernel(lx_ref, lg_ref, cw_ref, cb_ref, wr_ref, br_ref, wi_ref, bi_ref, lam_ref,
                o_ref, xs_ref, h_ref, *, tc):
    halo = SUBLANES

    @pl.when(pl.program_id(1) == 0)
    def _():
        xs_ref[0:halo, :] = jnp.zeros((halo, ATTN_W), F32)
        h_ref[...] = jnp.zeros_like(h_ref)

    x = lx_ref[0]
    xs_ref[halo:halo + tc, :] = x
    xb = cb_ref[...] + x * cw_ref[CONV_W - 1:CONV_W, :]
    for tap in range(CONV_W - 1):
        d = CONV_W - 1 - tap
        xb = xb + xs_ref[halo - d:halo - d + tc, :] * cw_ref[tap:tap + 1, :]
    xs_ref[0:halo, :] = x[tc - halo:tc, :]

    xbb = xb.astype(BF16)
    r = _sigmoid(jnp.dot(xbb, wr_ref[...], preferred_element_type=F32) + br_ref[...])
    ig = _sigmoid(jnp.dot(xbb, wi_ref[...], preferred_element_type=F32) + bi_ref[...])
    log_a = -LRU_C * r * _softplus(-lam_ref[...])
    a = jnp.exp(log_a)
    u = jnp.sqrt(jnp.tanh(-log_a) * (1.0 + a * a)) * (ig * xb)

    d = 1
    while d < tc:
        a_prev = _shift_rows(a, d, 1.0)
        u_prev = _shift_rows(u, d, 0.0)
        u = a * u_prev + u
        a = a * a_prev
        d *= 2
    h = a * h_ref[...] + u
    h_ref[...] = h[tc - 1:tc, :]
    o_ref[0] = h * _gelu_tanh(lg_ref[0])


def _lru(lx, lg, conv_w, conv_b, wr_bd, b_rg, wi_bd, b_ig, lam, tc):
    B, S, W = lx.shape
    row = lambda: pl.BlockSpec((1, W), lambda b, i: (0, 0))
    kernel = functools.partial(_lru_kernel, tc=tc)
    return pl.pallas_call(
        kernel,
        grid=(B, S // tc),
        in_specs=[
            pl.BlockSpec((1, tc, W), lambda b, i: (b, i, 0)),
            pl.BlockSpec((1, tc, W), lambda b, i: (b, i, 0)),
            pl.BlockSpec((CONV_W, W), lambda b, i: (0, 0)),
            row(),
            pl.BlockSpec((W, W), lambda b, i: (0, 0)),
            row(),
            pl.BlockSpec((W, W), lambda b, i: (0, 0)),
            row(), row(),
        ],
        out_specs=pl.BlockSpec((1, tc, W), lambda b, i: (b, i, 0)),
        out_shape=jax.ShapeDtypeStruct((B, S, W), F32),
        scratch_shapes=[pltpu.VMEM((tc + SUBLANES, W), F32), pltpu.VMEM((1, W), F32)],
        compiler_params=_cparams("arbitrary", "arbitrary"),
        name="lru",
    )(lx, lg, conv_w, conv_b, wr_bd, b_rg, wi_bd, b_ig, lam)


def _rms(v, g):
    ms = jnp.mean(v * v, axis=-1, keepdims=True)
    return v * lax.rsqrt(ms + EPS) * g


def _out_proj_kernel(at_ref, rc_ref, x_ref, ga_ref, gr_ref, wo_ref, g1_ref, g2_ref,
                     sc_ref, sh_ref, wq_ref, x1_ref, h2_ref, qp_ref):
    ma = _rms(at_ref[0], ga_ref[...]).astype(BF16)
    mr = _rms(rc_ref[0], gr_ref[...]).astype(BF16)
    y = (jnp.dot(ma, wo_ref[0:ATTN_W, :], preferred_element_type=F32)
         + jnp.dot(mr, wo_ref[ATTN_W:2 * ATTN_W, :], preferred_element_type=F32))
    x1 = x_ref[0] + g1_ref[0] * y
    x1_ref[0] = x1
    h2 = _rms(x1, g2_ref[...]) * (1.0 + sc_ref[0]) + sh_ref[0]
    h2_ref[0] = h2
    qp_ref[0] = jnp.dot(h2.astype(BF16), wq_ref[...], preferred_element_type=F32).astype(BF16)


def _out_proj(attn, rec, x, g_oa, g_ol, w_out_bf, gate1, g2, scale2, shift2, w_pq_bf, ts):
    B, S, D = x.shape
    W = attn.shape[-1]
    tok = lambda w: pl.BlockSpec((1, ts, w), lambda b, i: (b, i, 0))
    vec = lambda: pl.BlockSpec((1, 1, D), lambda b, i: (b, 0, 0))
    full = lambda r, c: pl.BlockSpec((r, c), lambda b, i: (0, 0))
    return pl.pallas_call(
        _out_proj_kernel,
        grid=(B, S // ts),
        in_specs=[tok(W), tok(W), tok(D), full(1, W), full(1, W), full(D, D), vec(),
                  full(1, D), vec(), vec(), full(D, D)],
        out_specs=[tok(D), tok(D), tok(D)],
        out_shape=[jax.ShapeDtypeStruct((B, S, D), F32),
                   jax.ShapeDtypeStruct((B, S, D), F32),
                   jax.ShapeDtypeStruct((B, S, D), BF16)],
        compiler_params=_cparams("arbitrary", "arbitrary"),
        name="out_proj",
    )(attn, rec, x, g_oa, g_ol, w_out_bf, gate1, g2, scale2, shift2, w_pq_bf)


def _tree(xs, op):
    xs = list(xs)
    while len(xs) > 1:
        nxt = [op(xs[k], xs[k + 1]) for k in range(0, len(xs) - 1, 2)]
        if len(xs) % 2:
            nxt.append(xs[-1])
        xs = nxt
    return xs[0]


def _all_sublanes(x, op):
    for s in (4, 2, 1):
        x = op(x, pltpu.roll(x, s, axis=0))
    return x


def _top_of_pairs(problems, rounds):
    neg_inf = jnp.float32(-jnp.inf)
    shape = problems[0][0][0][0].shape
    row8 = lax.broadcasted_iota(I32, shape, 0)
    big = jnp.int32(TOPK * TOPK)
    levels = len(problems[0][0])
    mv = [[t[0] for t in main] for main, _ in problems]
    me = [[t[1] for t in main] for main, _ in problems]
    sv = [[t[0] for t in side] for _, side in problems]
    se = [[t[1] for t in side] for _, side in problems]
    sp = [[t[2] for t in side] for _, side in problems]
    mpos = [row8 * TOPK for _ in problems]
    out_v = [[jnp.zeros(shape, F32) for _ in range(rounds // SUBLANES)] for _ in problems]
    out_e = [[jnp.zeros(shape, I32) for _ in range(rounds // SUBLANES)] for _ in problems]
    for r in range(rounds):
        here = row8 == (r % SUBLANES)
        depth = min(rounds - r, levels)
        for p in range(len(problems)):
            heads = [mv[p][0]] + sv[p]
            poss = [mpos[p]] + sp[p]
            eids = [me[p][0]] + se[p]
            m = _all_sublanes(_tree(heads, jnp.maximum), jnp.maximum)
            first = [jnp.where(h == m, q, big) for h, q in zip(heads, poss)]
            sel = _all_sublanes(_tree(first, jnp.minimum), jnp.minimum)
            wins = [(q == sel) & (h == m) for q, h in zip(poss, heads)]
            taken = _all_sublanes(
                _tree([jnp.where(w, e, -1) for w, e in zip(wins, eids)], jnp.maximum), jnp.maximum)
            out_v[p][r // SUBLANES] = jnp.where(here, m, out_v[p][r // SUBLANES])
            out_e[p][r // SUBLANES] = jnp.where(here, taken, out_e[p][r // SUBLANES])
            for k in range(len(sv[p])):
                sv[p][k] = jnp.where(wins[k + 1], neg_inf, sv[p][k])
            win = wins[0]
            mpos[p] = jnp.where(win, mpos[p] + 1, mpos[p])
            for lvl in range(depth - 1):
                mv[p][lvl] = jnp.where(win, mv[p][lvl + 1], mv[p][lvl])
                me[p][lvl] = jnp.where(win, me[p][lvl + 1], me[p][lvl])
            if depth == levels:
                mv[p][levels - 1] = jnp.where(win, neg_inf, mv[p][levels - 1])
    return [(out_v[p], out_e[p]) for p in range(len(problems))]


def _sorting_network(n):
    pairs = []
    p = 1
    while p < n:
        k = p
        while k >= 1:
            for j in range(k % p, n - k, 2 * k):
                for i in range(min(k, n - j - k)):
                    if (i + j) // (2 * p) == (i + j + k) // (2 * p):
                        pairs.append((i + j, i + j + k))
            k //= 2
        p *= 2
    return pairs


def _top_of_keys(problems, ids, rounds):
    n = len(ids)
    shape = ids[0].shape
    row8 = lax.broadcasted_iota(I32, shape, 0)
    big = jnp.int32(n * SUBLANES)
    vals = [list(v) for v in problems]
    keys = [list(ids) for _ in problems]
    for i, j in _sorting_network(n):
        for v, k in zip(vals, keys):
            swap = (v[j] > v[i]) | ((v[j] == v[i]) & (k[j] < k[i]))
            v[i], v[j] = jnp.where(swap, v[j], v[i]), jnp.where(swap, v[i], v[j])
            k[i], k[j] = jnp.where(swap, k[j], k[i]), jnp.where(swap, k[i], k[j])
    out_v = [[jnp.zeros(shape, F32) for _ in range(rounds // SUBLANES)] for _ in problems]
    out_k = [[jnp.zeros(shape, I32) for _ in range(rounds // SUBLANES)] for _ in problems]
    for r in range(rounds):
        here = row8 == (r % SUBLANES)
        depth = min(rounds - r, n)
        for p, (v, k) in enumerate(zip(vals, keys)):
            m = _all_sublanes(v[0], jnp.maximum)
            sel = _all_sublanes(jnp.where(v[0] == m, k[0], big), jnp.minimum)
            win = k[0] == sel
            out_v[p][r // SUBLANES] = jnp.where(here, m, out_v[p][r // SUBLANES])
            out_k[p][r // SUBLANES] = jnp.where(here, sel, out_k[p][r // SUBLANES])
            for lvl in range(depth - 1):
                v[lvl] = jnp.where(win, v[lvl + 1], v[lvl])
                k[lvl] = jnp.where(win, k[lvl + 1], k[lvl])
    return [(out_v[p], out_k[p]) for p in range(len(problems))]


def _row(x, r):
    return jnp.broadcast_to(x[r:r + 1, :], x.shape)


def _candidates(v1, v2, i1, i2):
    neg_inf = jnp.float32(-jnp.inf)
    row8 = lax.broadcasted_iota(I32, v1[0].shape, 0)
    main = []
    for b in range(SUBLANES):
        s = v1[0] + _row(v2[0], b)
        n_valid = TOPK // (b + 1)
        if n_valid < SUBLANES:
            s = jnp.where(row8 < n_valid, s, neg_inf)
        main.append((s, i1[0] * N_KEYS + _row(i2[0], b)))
    side = [
        (v1[1] + _row(v2[0], 0), i1[1] * N_KEYS + _row(i2[0], 0), (row8 + SUBLANES) * TOPK),
        (_row(v1[0], 0) + v2[1], _row(i1[0], 0) * N_KEYS + i2[1], row8 + SUBLANES),
    ]
    return main, side


def _topk_kernel(qp_ref, kc_ref, idx_ref, g_ref, e_scr, g_scr, *, tt):
    row8 = lax.broadcasted_iota(I32, (SUBLANES, tt), 0)
    kid = [row8 + k * SUBLANES for k in range(N_KEYS // SUBLANES)]
    n_tiles = N_KEYS // SUBLANES

    def head_pair(hp, c):
        cands = []
        for k in range(2):
            h = 2 * hp + k
            c0 = pl.multiple_of(h * PEER_DK, PEER_DK)
            qh = qp_ref[:, pl.ds(c0, PEER_DK)]
            st = lax.dot_general(kc_ref[h], qh, (((1,), (1,)), ((), ())),
                                 preferred_element_type=F32)
            tiles = [st[j * SUBLANES:(j + 1) * SUBLANES, :] for j in range(2 * n_tiles)]
            (v1, i1), (v2, i2) = _top_of_keys([tiles[:n_tiles], tiles[n_tiles:]], kid, TOPK)
            cands.append(_candidates(v1, v2, i1, i2))
        tops = _top_of_pairs(cands, TOPK)
        for k, (top_s, top_e) in enumerate(tops):
            best = _row(top_s[0], 0)
            ex = [jnp.exp(t - best) for t in top_s]
            denom = _all_sublanes(ex[0] + ex[1], jnp.add)
            r0 = (2 * hp + k) * TOPK
            for j in range(TOPK // SUBLANES):
                rj = pl.multiple_of(r0 + j * SUBLANES, SUBLANES)
                e_scr[pl.ds(rj, SUBLANES), :] = top_e[j] * 4
                g_scr[pl.ds(rj, SUBLANES), :] = ex[j] / denom
        return c

    lax.fori_loop(0, PEER_HEADS // 2, head_pair, 0)
    idx_ref[...] = e_scr[...].T
    g_ref[...] = g_scr[...].T


def _topk(qp2, kc, tt):
    N, D = qp2.shape
    kernel = functools.partial(_topk_kernel, tt=tt)
    slots = PEER_HEADS * TOPK
    return pl.pallas_call(
        kernel,
        grid=(N // tt,),
        in_specs=[
            pl.BlockSpec((tt, D), lambda i: (i, 0)),
            pl.BlockSpec(kc.shape, lambda i: (0, 0, 0)),
        ],
        out_specs=[pl.BlockSpec((tt, slots), lambda i: (i, 0)),
                   pl.BlockSpec((tt, slots), lambda i: (i, 0))],
        out_shape=[jax.ShapeDtypeStruct((N, slots), I32),
                   jax.ShapeDtypeStruct((N, slots), F32)],
        scratch_shapes=[pltpu.VMEM((slots, tt), I32), pltpu.VMEM((slots, tt), F32)],
        compiler_params=_cparams("arbitrary"),
        name="topk",
    )(qp2, kc)


ROWS = PEER_HEADS * TOPK
GROUPS = ROWS // SUBLANES
CHUNKS = 4
IDX_PARTS = 1
PART_ROWS = ROWS // IDX_PARTS


def _gather_rows(idx_refs, t, tab_ref, gbuf):
    base = t * PART_ROWS
    for k in range(PART_ROWS):
        off = base + k
        for p, idx_ref in enumerate(idx_refs):
            grp, j = divmod(p * PART_ROWS + k, SUBLANES)
            e = idx_ref[off]
            gbuf[grp, pl.ds(j, CHUNKS, stride=SUBLANES), :] = (
                tab_ref[pl.ds(pl.multiple_of(e, CHUNKS), CHUNKS), :])


def _idx_specs(tb):
    return [pl.BlockSpec((tb * PART_ROWS,), lambda i: (i,), memory_space=pltpu.SMEM)
            for _ in range(IDX_PARTS)]


def _unpack(w):
    lo = pltpu.bitcast(w << 16, F32)
    hi = pltpu.bitcast(w & jnp.int32(-65536), F32)
    return lo, hi


def _peer_u_kernel(*refs, tb):
    idx_ref = refs[:IDX_PARTS]
    x_ref, g_ref, tab_ref, w_ref, gbuf0, gbuf1, q_scr = refs[IDX_PARTS:]
    ones = jnp.ones((SUBLANES, LANES), BF16)
    lane_dims = (((1,), (1,)), ((), ()))

    def dots(t, gbuf, slot):
        xrow = x_ref[pl.ds(t, 1), :]
        xb = [jnp.broadcast_to(xrow[:, c * LANES:(c + 1) * LANES], (SUBLANES, LANES))
              for c in range(2 * CHUNKS)]
        for grp in range(GROUPS):
            acc = None
            for c in range(CHUNKS):
                lo, hi = _unpack(gbuf[grp, pl.ds(c * SUBLANES, SUBLANES), :])
                p = lo * xb[c] + hi * xb[c + CHUNKS]
                acc = p if acc is None else acc + p
            q_scr[slot, pl.ds(grp * SUBLANES, SUBLANES), :] = acc

    def lane_sums(slot):
        qh, ql = _split_bf16(q_scr[slot])
        s = (lax.dot_general(ones, qh, lane_dims, preferred_element_type=F32)
             + lax.dot_general(ones, ql, lane_dims, preferred_element_type=F32))
        return s[0:1, :]

    def emit(t, s):
        w_ref[pl.ds(t, 1), :] = _gelu_tanh(s) * g_ref[pl.ds(t, 1), :]

    q_scr[...] = jnp.zeros_like(q_scr)
    _gather_rows(idx_ref, 0, tab_ref, gbuf0)

    def pair(i, c):
        t0 = 2 * i
        tp = jnp.maximum(t0 - 2, 0)
        s0 = lane_sums(0)
        s1 = lane_sums(1)
        _gather_rows(idx_ref, t0 + 1, tab_ref, gbuf1)
        dots(t0, gbuf0, 0)
        _gather_rows(idx_ref, jnp.minimum(t0 + 2, tb - 1), tab_ref, gbuf0)
        dots(t0 + 1, gbuf1, 1)
        emit(tp, s0)
        emit(tp + 1, s1)
        return c

    lax.fori_loop(0, tb // 2, pair, 0)
    emit(tb - 2, lane_sums(0))
    emit(tb - 1, lane_sums(1))


def _peer_u(idx_heads, h2, g, tab, tb):
    N, D = h2.shape
    kernel = functools.partial(_peer_u_kernel, tb=tb)
    return pl.pallas_call(
        kernel,
        grid=(N // tb,),
        in_specs=_idx_specs(tb) + [
            pl.BlockSpec((tb, D), lambda i: (i, 0)),
            pl.BlockSpec((tb, ROWS), lambda i: (i, 0)),
            pl.BlockSpec(memory_space=pltpu.VMEM),
        ],
        out_specs=pl.BlockSpec((tb, ROWS), lambda i: (i, 0)),
        out_shape=jax.ShapeDtypeStruct((N, ROWS), F32),
        scratch_shapes=[pltpu.VMEM((GROUPS, CHUNKS * SUBLANES, LANES), I32),
                        pltpu.VMEM((GROUPS, CHUNKS * SUBLANES, LANES), I32),
                        pltpu.VMEM((2, ROWS, LANES), F32)],
        compiler_params=_cparams("arbitrary"),
        name="peer_u",
    )(*idx_heads, h2, g, tab)


def _peer_v_kernel(*refs, tb):
    idx_ref = refs[:IDX_PARTS]
    w_ref, x1_ref, g2_ref, tab_ref, o_ref, gbuf0, gbuf1 = refs[IDX_PARTS:]
    def combine(t, gbuf):
        wrow = w_ref[pl.ds(t, 1), :]
        wb = jnp.broadcast_to(wrow, (ROWS, ROWS)).T
        acc_lo = [None] * CHUNKS
        acc_hi = [None] * CHUNKS
        for grp in range(GROUPS):
            wg = wb[grp * SUBLANES:(grp + 1) * SUBLANES, :]
            for c in range(CHUNKS):
                lo, hi = _unpack(gbuf[grp, pl.ds(c * SUBLANES, SUBLANES), :])
                pl_, ph_ = lo * wg, hi * wg
                acc_lo[c] = pl_ if acc_lo[c] is None else acc_lo[c] + pl_
                acc_hi[c] = ph_ if acc_hi[c] is None else acc_hi[c] + ph_
        y = jnp.concatenate([jnp.sum(a, axis=0, keepdims=True) for a in acc_lo + acc_hi], axis=1)
        o_ref[pl.ds(t, 1), :] = x1_ref[pl.ds(t, 1), :] + g2_ref[0] * y

    _gather_rows(idx_ref, 0, tab_ref, gbuf0)

    def pair(i, c):
        t0 = 2 * i
        _gather_rows(idx_ref, t0 + 1, tab_ref, gbuf1)
        combine(t0, gbuf0)
        _gather_rows(idx_ref, jnp.minimum(t0 + 2, tb - 1), tab_ref, gbuf0)
        combine(t0 + 1, gbuf1)
        return c

    lax.fori_loop(0, tb // 2, pair, 0)


def _peer_v(idx_heads, w, x1, gate2, tab, tb, seq):
    N, D = x1.shape
    kernel = functools.partial(_peer_v_kernel, tb=tb)
    per_seq = seq // tb
    return pl.pallas_call(
        kernel,
        grid=(N // tb,),
        in_specs=_idx_specs(tb) + [
            pl.BlockSpec((tb, ROWS), lambda i: (i, 0)),
            pl.BlockSpec((tb, D), lambda i: (i, 0)),
            pl.BlockSpec((1, 1, D), lambda i: (i // per_seq, 0, 0)),
            pl.BlockSpec(memory_space=pltpu.VMEM),
        ],
        out_specs=pl.BlockSpec((tb, D), lambda i: (i, 0)),
        out_shape=jax.ShapeDtypeStruct((N, D), F32),
        scratch_shapes=[pltpu.VMEM((GROUPS, CHUNKS * SUBLANES, LANES), I32),
                        pltpu.VMEM((GROUPS, CHUNKS * SUBLANES, LANES), I32)],
        compiler_params=_cparams("arbitrary"),
        name="peer_v",
    )(*idx_heads, w, x1, gate2, tab)


def _pack_table(tab):
    E, D = tab.shape
    half = D // 2
    bits = lax.bitcast_convert_type(tab.astype(BF16), jnp.uint16).astype(jnp.uint32)
    word = bits[:, :half] | (bits[:, half:] << 16)
    return lax.bitcast_convert_type(word, I32).reshape(E * (half // LANES), LANES)


def _block_diag(w):
    nb, n, _ = w.shape
    eye = jnp.eye(nb, dtype=w.dtype)
    return (eye[:, None, :, None] * w[:, :, None, :]).reshape(nb * n, nb * n)


def _key_matrix(k1, k2):
    z = jnp.zeros_like(k1)
    return jnp.concatenate([jnp.concatenate([k1, z], -1), jnp.concatenate([z, k2], -1)], axis=1)


def kernel(x, c, w_mod, b_mod, g_norm1, w_in, g_q, g_k, conv_w, conv_b, w_rg, b_rg, w_ig, b_ig,
           lru_lambda, g_out_attn, g_out_lru, w_out, g_norm2, w_pq, sub_keys1, sub_keys2,
           expert_u, expert_v):
    B, S, D = x.shape
    N = B * S
    depth = w_mod.shape[0]
    assert ATTN_W == 4 * LANES and D == 2 * ATTN_W
    ts = min(512, S)
    tq, tk = 128, min(512, S)
    tc = min(512, S)
    tt = 256
    tb = 128
    h_res = x.astype(F32)
    for l in range(depth):
        mod = _mod(c.astype(F32), w_mod[l].astype(F32), b_mod[l].astype(F32))
        shift1, scale1, gate1, shift2, scale2, gate2 = [
            m.reshape(B, 1, D) for m in jnp.split(mod, 6, axis=-1)]

        scale = 1.0 / math.sqrt(HEAD_DIM)
        gq_t = jnp.tile(g_q[l].astype(F32) * scale, ATTN_HEADS).reshape(1, ATTN_W)
        gk_t = jnp.tile(g_k[l].astype(F32), ATTN_HEADS).reshape(1, ATTN_W)
        qkv, lx, lg = _in_proj(h_res, g_norm1[l].astype(F32).reshape(1, D), scale1, shift1,
                               w_in[l].astype(BF16), gq_t, gk_t, ts)

        attn = _attention(qkv, tq, tk)
        W = lx.shape[-1]
        rec = _lru(lx, lg, conv_w[l].astype(F32), conv_b[l].astype(F32).reshape(1, W),
                   _block_diag(w_rg[l]).astype(BF16), b_rg[l].astype(F32).reshape(1, W),
                   _block_diag(w_ig[l]).astype(BF16), b_ig[l].astype(F32).reshape(1, W),
                   lru_lambda[l].astype(F32).reshape(1, W), tc)

        x1, h2, qp = _out_proj(attn, rec, h_res,
                               g_out_attn[l].astype(F32).reshape(1, W),
                               g_out_lru[l].astype(F32).reshape(1, W),
                               w_out[l].astype(BF16), gate1,
                               g_norm2[l].astype(F32).reshape(1, D), scale2, shift2,
                               w_pq[l].astype(BF16), ts)

        kc = _key_matrix(sub_keys1[l], sub_keys2[l]).astype(BF16)
        idx, g = _topk(qp.reshape(N, D), kc, tt)
        idx_heads = [idx[:, p * PART_ROWS:(p + 1) * PART_ROWS].reshape(N * PART_ROWS)
                     for p in range(IDX_PARTS)]
        wts = _peer_u(idx_heads, h2.reshape(N, D), g, _pack_table(expert_u[l]), tb)
        out = _peer_v(idx_heads, wts, x1.reshape(N, D), gate2, _pack_table(expert_v[l]), tb, S)
        h_res = out.reshape(B, S, D)
    return h_res.astype(x.dtype)
```
